```python
import jax, jax.numpy as jnp
from jax import lax
import numpy as np

D_MODEL = 2048
BATCH = 4
SEQ = 4096
DEPTH = 4

N_MIXERS = 3
N_RWKV = (DEPTH + 2) // 3
N_SWA = (DEPTH + 1) // 3
N_CONV = DEPTH // 3
RMS_EPS = 1e-6

RWKV_HEAD = 64
RWKV_HEADS = D_MODEL // RWKV_HEAD
DECAY_LORA = 96
AAA_LORA = 96
MV_LORA = 64
GATE_LORA = 256
GN_EPS = 64e-5
N_MIX_COEF = 6

ATT_HEAD = 64
N_Q_HEADS = D_MODEL // ATT_HEAD
N_KV_HEADS = max(4, N_Q_HEADS // 8)
Q_PER_KV = N_Q_HEADS // N_KV_HEADS
WINDOW = 128
BLOCK = 128

CONV_W = 3

D_FF = 4 * D_MODEL

kernel_name = "hybrid_rwkv7_swa_sink_shortconv_trunk"


def rms_norm(x, g):
    xf = x.astype(jnp.float32)
    y = xf * lax.rsqrt(jnp.mean(xf * xf, axis=-1, keepdims=True) + RMS_EPS)
    return (y * g.astype(jnp.float32)).astype(x.dtype)


def token_shift(x):
    return jnp.pad(x, ((0, 0), (1, 0), (0, 0)))[:, :-1]


def rwkv7_scan(r, w, k, v, a, b):
    bsz, _, h, n = r.shape

    def step(S, inp):
        r_t, w_t, k_t, v_t, a_t, b_t = inp
        sa = jnp.einsum('bhij,bhj->bhi', S, a_t)
        S = S * w_t[:, :, None, :] + sa[..., None] * b_t[:, :, None, :] + v_t[..., None] * k_t[:, :, None, :]
        return S, jnp.einsum('bhij,bhj->bhi', S, r_t)

    S0 = jnp.zeros((bsz, h, n, n), jnp.float32)
    seq = tuple(jnp.moveaxis(t, 1, 0) for t in (r, w, k, v, a, b))
    _, y = lax.scan(step, S0, seq)
    return jnp.moveaxis(y, 0, 1)


def rwkv7_time_mix(x, v_first, mu, w_rkv, w_o, w0, w1, w2, a0, a1, a2, g1, g2,
                   k_k, k_a, r_k, lnx_w, lnx_b, vres):
    B, T, D = x.shape
    H, N = RWKV_HEADS, RWKV_HEAD
    f32 = jnp.float32
    xx = token_shift(x) - x
    xr, xw, xk, xv, xa, xg = [x + xx * mu[i] for i in range(N_MIX_COEF)]
    r, k, v = jnp.einsum('cbtd,cde->cbte', jnp.stack([xr, xk, xv]), w_rkv)
    w = -jax.nn.softplus(-(w0 + jnp.tanh(xw @ w1) @ w2).astype(f32)) - 0.5
    decay = jnp.exp(-jnp.exp(w))
    if vres is None:
        v_first = v
    else:
        v0, v1, v2 = vres
        v = v + (v_first - v) * jax.nn.sigmoid(v0 + (xv @ v1) @ v2)
    a = jax.nn.sigmoid(a0 + (xa @ a1) @ a2)
    g = jax.nn.sigmoid(xg @ g1) @ g2
    kk = (k * k_k).reshape(B, T, H, N).astype(f32)
    kk = kk / jnp.maximum(jnp.linalg.norm(kk, axis=-1, keepdims=True), 1e-12)
    k = k * (1.0 + (a - 1.0) * k_a)
    rh = r.reshape(B, T, H, N).astype(f32)
    kh = k.reshape(B, T, H, N).astype(f32)
    vh = v.reshape(B, T, H, N).astype(f32)
    ah = a.reshape(B, T, H, N).astype(f32)
    y = rwkv7_scan(rh, decay.reshape(B, T, H, N), kh, vh, -kk, kk * ah)
    mean = jnp.mean(y, axis=-1, keepdims=True)
    var = jnp.mean(jnp.square(y - mean), axis=-1, keepdims=True)
    y = (y - mean) * lax.rsqrt(var + GN_EPS)
    y = y.reshape(B, T, D) * lnx_w.astype(f32) + lnx_b.astype(f32)
    bonus = jnp.sum(rh * kh * r_k.astype(f32), axis=-1, keepdims=True) * vh
    y = (y + bonus.reshape(B, T, D)).astype(x.dtype)
    return (y * g) @ w_o, v_first


def swa_sink_attention(x, w_qkv, b_qkv, w_o, b_o, sinks):
    B, T, D = x.shape
    nb = T // BLOCK
    kvd = N_KV_HEADS * ATT_HEAD
    qkv = x @ w_qkv + b_qkv
    q = qkv[..., :D].reshape(B, nb, BLOCK, N_KV_HEADS, Q_PER_KV, ATT_HEAD) * (ATT_HEAD ** -0.5)
    k = qkv[..., D:D + kvd].reshape(B, nb, BLOCK, N_KV_HEADS, ATT_HEAD)
    v = qkv[..., D + kvd:].reshape(B, nb, BLOCK, N_KV_HEADS, ATT_HEAD)

    def with_prev(t):
        prev = jnp.pad(t, ((0, 0), (1, 0), (0, 0), (0, 0), (0, 0)))[:, :-1]
        return jnp.concatenate([prev, t], axis=2)

    kb, vb = with_prev(k), with_prev(v)
    logits = jnp.einsum('bnqhgd,bnkhd->bnhgqk', q, kb).astype(jnp.float32)
    blk = jnp.arange(nb)[:, None, None] * BLOCK
    qpos = blk + jnp.arange(BLOCK)[None, :, None]
    kpos = blk - BLOCK + jnp.arange(2 * BLOCK)[None, None, :]
    rel = qpos - kpos
    mask = (rel >= 0) & (rel < WINDOW) & (kpos >= 0)
    logits = jnp.where(mask[None, :, None, None], logits, -jnp.inf)
    sink = sinks.astype(jnp.float32).reshape(N_KV_HEADS, Q_PER_KV)[None, None, :, :, None]
    m = jnp.maximum(jnp.max(logits, axis=-1), sink)
    p = jnp.exp(logits - m[..., None])
    denom = jnp.sum(p, axis=-1) + jnp.exp(sink - m)
    probs = (p / denom[..., None]).astype(x.dtype)
    o = jnp.einsum('bnhgqk,bnkhd->bnqhgd', probs, vb).reshape(B, T, D)
    return o @ w_o + b_o


def short_gated_conv(x, w_in, conv_w, w_out):
    D = x.shape[-1]
    bch = x @ w_in
    bg, cg, h = bch[..., :D], bch[..., D:2 * D], bch[..., 2 * D:]
    u = cg * h
    uc = lax.conv_general_dilated(u, conv_w[:, None, :], window_strides=(1,),
                                  padding=((CONV_W - 1, 0),),
                                  dimension_numbers=('NWC', 'WIO', 'NWC'),
                                  feature_group_count=D)
    return (bg * uc) @ w_out


def sqrelu_mlp(x, w_up, w_down):
    h = jax.nn.relu(x @ w_up)
    return (h * h) @ w_down


def setup_inputs(seed: int = 0) -> dict:
    key = jax.random.key(seed)
    ks = iter(jax.random.split(key, 40))
    nrm = lambda shape, s: jax.random.normal(next(ks), shape, jnp.float32) * s
    uni = lambda shape, lo, hi: jax.random.uniform(next(ks), shape, jnp.float32, lo, hi)
    D, H, N = D_MODEL, RWKV_HEADS, RWKV_HEAD
    dsc = D ** -0.5
    kvd = N_KV_HEADS * ATT_HEAD
    return {
        "x": nrm((BATCH, SEQ, D), 1.0),
        "norm_mix": 1.0 + nrm((DEPTH, D), 0.02),
        "norm_ffn": 1.0 + nrm((DEPTH, D), 0.02),
        "norm_final": 1.0 + nrm((D,), 0.02),
        "rwkv_mu": uni((N_RWKV, N_MIX_COEF, D), 0.0, 1.0),
        "rwkv_w_rkv": nrm((N_RWKV, 3, D, D), dsc),
        "rwkv_w_o": nrm((N_RWKV, D, D), dsc),
        "rwkv_w0": uni((N_RWKV, D), -6.0, -1.0),
        "rwkv_w1": nrm((N_RWKV, D, DECAY_LORA), dsc),
        "rwkv_w2": nrm((N_RWKV, DECAY_LORA, D), 0.1 * DECAY_LORA ** -0.5),
        "rwkv_a0": nrm((N_RWKV, D), 0.5),
        "rwkv_a1": nrm((N_RWKV, D, AAA_LORA), dsc),
        "rwkv_a2": nrm((N_RWKV, AAA_LORA, D), 0.5 * AAA_LORA ** -0.5),
        "rwkv_v0": nrm((N_RWKV - 1, D), 0.5),
        "rwkv_v1": nrm((N_RWKV - 1, D, MV_LORA), dsc),
        "rwkv_v2": nrm((N_RWKV - 1, MV_LORA, D), 0.5 * MV_LORA ** -0.5),
        "rwkv_g1": nrm((N_RWKV, D, GATE_LORA), dsc),
        "rwkv_g2": nrm((N_RWKV, GATE_LORA, D), GATE_LORA ** -0.5),
        "rwkv_k_k": 1.0 + nrm((N_RWKV, D), 0.1),
        "rwkv_k_a": uni((N_RWKV, D), 0.0, 1.0),
        "rwkv_r_k": nrm((N_RWKV, H, N), 0.1),
        "rwkv_lnx_w": 1.0 + nrm((N_RWKV, D), 0.02),
        "rwkv_lnx_b": nrm((N_RWKV, D), 0.02),
        "swa_w_qkv": nrm((N_SWA, D, D + 2 * kvd), dsc),
        "swa_b_qkv": nrm((N_SWA, D + 2 * kvd), 0.02),
        "swa_w_o": nrm((N_SWA, D, D), dsc),
        "swa_b_o": nrm((N_SWA, D), 0.02),
        "swa_sinks": nrm((N_SWA, N_Q_HEADS), 1.0),
        "conv_w_in": nrm((N_CONV, D, 3 * D), dsc),
        "conv_w": nrm((N_CONV, CONV_W, D), CONV_W ** -0.5),
        "conv_w_out": nrm((N_CONV, D, D), dsc),
        "mlp_w_up": nrm((DEPTH, D, D_FF), dsc),
        "mlp_w_down": nrm((DEPTH, D_FF, D), D_FF ** -0.5),
    }


def reference(x, norm_mix, norm_ffn, norm_final,
              rwkv_mu, rwkv_w_rkv, rwkv_w_o, rwkv_w0, rwkv_w1, rwkv_w2,
              rwkv_a0, rwkv_a1, rwkv_a2, rwkv_v0, rwkv_v1, rwkv_v2,
              rwkv_g1, rwkv_g2, rwkv_k_k, rwkv_k_a, rwkv_r_k, rwkv_lnx_w, rwkv_lnx_b,
              swa_w_qkv, swa_b_qkv, swa_w_o, swa_b_o, swa_sinks,
              conv_w_in, conv_w, conv_w_out,
              mlp_w_up, mlp_w_down):
    v_first = None
    ia = ib = ic = 0
    for i in range(DEPTH):
        h = rms_norm(x, norm_mix[i])
        kind = i % N_MIXERS
        if kind == 0:
            vres = None if ia == 0 else (rwkv_v0[ia - 1], rwkv_v1[ia - 1], rwkv_v2[ia - 1])
            out, v_first = rwkv7_time_mix(
                h, v_first, rwkv_mu[ia], rwkv_w_rkv[ia], rwkv_w_o[ia],
                rwkv_w0[ia], rwkv_w1[ia], rwkv_w2[ia],
                rwkv_a0[ia], rwkv_a1[ia], rwkv_a2[ia],
                rwkv_g1[ia], rwkv_g2[ia], rwkv_k_k[ia], rwkv_k_a[ia], rwkv_r_k[ia],
                rwkv_lnx_w[ia], rwkv_lnx_b[ia], vres)
            ia += 1
        elif kind == 1:
            out = swa_sink_attention(h, swa_w_qkv[ib], swa_b_qkv[ib], swa_w_o[ib],
                                     swa_b_o[ib], swa_sinks[ib])
            ib += 1
        else:
            out = short_gated_conv(h, conv_w_in[ic], conv_w[ic], conv_w_out[ic])
            ic += 1
        x = x + out
        x = x + sqrelu_mlp(rms_norm(x, norm_ffn[i]), mlp_w_up[i], mlp_w_down[i])
    return rms_norm(x, norm_final)
```

```python
import functools

import jax
import jax.numpy as jnp
from jax import lax
from jax.experimental import pallas as pl
from jax.experimental.pallas import tpu as pltpu

F32 = jnp.float32
BF16 = jnp.bfloat16

HEAD = 64
LANES = 128
SUBLANES = 8
CHUNK = 64
ATT_BLOCK = 128
RMS_EPS = 1e-6
GN_EPS = 64e-5
DECAY_SCALE = 0.6065306597126334
VMEM_LIMIT = 56 * 1024 * 1024

_NT = (((1,), (1,)), ((), ()))
_TN = (((0,), (0,)), ((), ()))


def _tile(n, pref):
    t = min(n, pref)
    while n % t:
        t //= 2
    return t


def _params(sem):
    return pltpu.CompilerParams(dimension_semantics=sem, vmem_limit_bytes=VMEM_LIMIT)


def _rms(x, g):
    ms = jnp.mean(x * x, axis=-1, keepdims=True)
    return x * lax.rsqrt(ms + RMS_EPS) * g


def _dot(a, b):
    return jnp.dot(a, b, preferred_element_type=F32)


def _dot_hl(x, ones_bf16):
    hi = x.astype(BF16)
    lo = (x - hi.astype(F32)).astype(BF16)
    return _dot(hi, ones_bf16) + _dot(lo, ones_bf16)


def _mm_kernel(*refs, has_norm, has_bias, has_res):
    it = iter(refs)
    x_ref = next(it)
    g_ref = next(it) if has_norm else None
    w_ref = next(it)
    b_ref = next(it) if has_bias else None
    r_ref = next(it) if has_res else None
    o_ref = next(it)
    xn_ref = next(it) if has_norm else None

    if has_norm:
        @pl.when(pl.program_id(1) == 0)
        def _():
            xn_ref[...] = _rms(x_ref[...], g_ref[...]).astype(BF16)
        xb = xn_ref[...]
    else:
        xb = x_ref[...]
    acc = _dot(xb, w_ref[...])
    if has_bias:
        acc = acc + b_ref[...]
    if has_res:
        acc = acc + r_ref[...]
    o_ref[...] = acc.astype(o_ref.dtype)


def _mm(x, w, *, norm_g=None, bias=None, res=None, out_dtype=F32, tm=512, tn=512, name="mm"):
    m, k = x.shape
    n = w.shape[1]
    tm, tn = _tile(m, tm), _tile(n, tn)
    has_norm, has_bias, has_res = norm_g is not None, bias is not None, res is not None
    args, specs = [x], [pl.BlockSpec((tm, k), lambda i, j: (i, 0))]
    if has_norm:
        args.append(norm_g.reshape(1, k))
        specs.append(pl.BlockSpec((1, k), lambda i, j: (0, 0)))
    args.append(w)
    specs.append(pl.BlockSpec((k, tn), lambda i, j: (0, j)))
    if has_bias:
        args.append(bias.reshape(1, n))
        specs.append(pl.BlockSpec((1, tn), lambda i, j: (0, j)))
    if has_res:
        args.append(res)
        specs.append(pl.BlockSpec((tm, tn), lambda i, j: (i, j)))
    return pl.pallas_call(
        functools.partial(_mm_kernel, has_norm=has_norm, has_bias=has_bias, has_res=has_res),
        out_shape=jax.ShapeDtypeStruct((m, n), out_dtype),
        grid=(m // tm, n // tn),
        in_specs=specs,
        out_specs=pl.BlockSpec((tm, tn), lambda i, j: (i, j)),
        scratch_shapes=[pltpu.VMEM((tm, k), BF16)] if has_norm else [],
        compiler_params=_params(("parallel", "arbitrary")),
        name=name,
    )(*args)


def _ffn_kernel(*refs, has_final):
    if has_final:
        x_ref, g_ref, wu_ref, wd_ref, gf_ref, o_ref, xn_ref = refs
    else:
        x_ref, g_ref, wu_ref, wd_ref, o_ref, xn_ref = refs
    f = pl.program_id(1)

    @pl.when(f == 0)
    def _():
        x = x_ref[...]
        xn_ref[...] = _rms(x, g_ref[...]).astype(BF16)
        o_ref[...] = x

    h = jnp.maximum(_dot(xn_ref[...], wu_ref[...]), 0.0)
    o_ref[...] += _dot((h * h).astype(BF16), wd_ref[...])

    if has_final:
        @pl.when(f == pl.num_programs(1) - 1)
        def _():
            o_ref[...] = _rms(o_ref[...], gf_ref[...])


def _ffn(x, g, w_up, w_down, final_g=None, *, tm=512, tf=512):
    m, d = x.shape
    ff = w_up.shape[1]
    tm, tf = _tile(m, tm), _tile(ff, tf)
    has_final = final_g is not None
    args = [x, g.reshape(1, d), w_up, w_down]
    specs = [
        pl.BlockSpec((tm, d), lambda i, f: (i, 0)),
        pl.BlockSpec((1, d), lambda i, f: (0, 0)),
        pl.BlockSpec((d, tf), lambda i, f: (0, f)),
        pl.BlockSpec((tf, d), lambda i, f: (f, 0)),
    ]
    if has_final:
        args.append(final_g.reshape(1, d))
        specs.append(pl.BlockSpec((1, d), lambda i, f: (0, 0)))
    return pl.pallas_call(
        functools.partial(_ffn_kernel, has_final=has_final),
        out_shape=jax.ShapeDtypeStruct((m, d), F32),
        grid=(m // tm, ff // tf),
        in_specs=specs,
        out_specs=pl.BlockSpec((tm, d), lambda i, f: (i, 0)),
        scratch_shapes=[pltpu.VMEM((tm, d), BF16)],
        compiler_params=_params(("parallel", "arbitrary")),
        name="ffn",
    )(*args)


def _mix_kernel(x_ref, p_ref, g_ref, mu_ref, *o_refs, tiles_per_seq):
    g = g_ref[...]
    h = _rms(x_ref[...], g)
    hp = _rms(p_ref[...], g)[SUBLANES - 1:SUBLANES, :]
    hp = jnp.where(pl.program_id(0) % tiles_per_seq == 0, 0.0, hp)
    row = lax.broadcasted_iota(jnp.int32, h.shape, 0)
    shifted = jnp.where(row == 0, hp, pltpu.roll(h, 1, 0))
    xx = shifted - h
    for c, o_ref in enumerate(o_refs):
        o_ref[...] = (h + xx * mu_ref[c:c + 1, :]).astype(BF16)


def _rwkv_mix(x, g, mu, seq_len, *, tm=256):
    m, d = x.shape
    tm = _tile(seq_len, tm)
    ncoef = mu.shape[0]
    blk = tm // SUBLANES
    return pl.pallas_call(
        functools.partial(_mix_kernel, tiles_per_seq=seq_len // tm),
        out_shape=[jax.ShapeDtypeStruct((m, d), BF16)] * ncoef,
        grid=(m // tm,),
        in_specs=[
            pl.BlockSpec((tm, d), lambda i: (i, 0)),
            pl.BlockSpec((SUBLANES, d), lambda i: (jnp.maximum(i * blk - 1, 0), 0)),
            pl.BlockSpec((1, d), lambda i: (0, 0)),
            pl.BlockSpec((ncoef, d), lambda i: (0, 0)),
        ],
        out_specs=[pl.BlockSpec((tm, d), lambda i: (i, 0))] * ncoef,
        compiler_params=_params(("parallel",)),
        name="rwkv_mix",
    )(x, x, g.reshape(1, d), mu)


def _act(x, kind):
    if kind == "tanh":
        return jnp.tanh(x)
    if kind == "sigmoid":
        return jax.nn.sigmoid(x)
    return x


def _lora_kernel(*refs, mid_act, out_act, has_bias):
    if has_bias:
        x_ref, a_ref, b_ref, c_ref, o_ref = refs
    else:
        x_ref, a_ref, b_ref, o_ref = refs
    mid = _act(_dot(x_ref[...], a_ref[...]), mid_act).astype(BF16)
    out = _dot(mid, b_ref[...])
    if has_bias:
        out = out + c_ref[...]
    o_ref[...] = _act(out, out_act).astype(o_ref.dtype)


def _lora(x, a, b, bias=None, *, mid_act=None, out_act=None, tm=512):
    m, d = x.shape
    rank, n = b.shape
    pad = -rank % LANES
    if pad:
        a = jnp.pad(a, ((0, 0), (0, pad)))
        b = jnp.pad(b, ((0, pad), (0, 0)))
        rank += pad
    tm = _tile(m, tm)
    has_bias = bias is not None
    args = [x, a.astype(BF16), b.astype(BF16)]
    specs = [
        pl.BlockSpec((tm, d), lambda i: (i, 0)),
        pl.BlockSpec((d, rank), lambda i: (0, 0)),
        pl.BlockSpec((rank, n), lambda i: (0, 0)),
    ]
    if has_bias:
        args.append(bias.reshape(1, n))
        specs.append(pl.BlockSpec((1, n), lambda i: (0, 0)))
    return pl.pallas_call(
        functools.partial(_lora_kernel, mid_act=mid_act, out_act=out_act, has_bias=has_bias),
        out_shape=jax.ShapeDtypeStruct((m, n), F32),
        grid=(m // tm,),
        in_specs=specs,
        out_specs=pl.BlockSpec((tm, n), lambda i: (i, 0)),
        compiler_params=_params(("parallel",)),
        name="lora",
    )(*args)


def _scan_kernel(*refs, has_vres, n_chunks):
    if has_vres:
        (r_ref, k_ref, v_ref, w_ref, a_ref, g_ref, vf_ref, vg_ref,
         kk_ref, ka_ref, rk_ref, lw_ref, lb_ref, o_ref, ht_ref) = refs
    else:
        (r_ref, k_ref, v_ref, w_ref, a_ref, g_ref,
         kk_ref, ka_ref, rk_ref, lw_ref, lb_ref, o_ref, ht_ref) = refs
    c2 = 2 * CHUNK

    @pl.when(pl.program_id(2) == 0)
    def _():
        ht_ref[...] = jnp.zeros_like(ht_ref)

    lane = lax.broadcasted_iota(jnp.int32, (CHUNK, LANES), 1)
    head0 = lane < HEAD

    def stack(x):
        return jnp.concatenate([jnp.where(head0, x, 0.0), jnp.where(head0, 0.0, x)], axis=0).astype(BF16)

    ri = lax.broadcasted_iota(jnp.int32, (LANES, LANES), 0)
    ci = lax.broadcasted_iota(jnp.int32, (LANES, LANES), 1)
    head_ones = ((ri // HEAD) == (ci // HEAD)).astype(BF16)
    eye = (ri == ci).astype(F32)
    ti = lax.broadcasted_iota(jnp.int32, (CHUNK, CHUNK), 0)
    si = lax.broadcasted_iota(jnp.int32, (CHUNK, CHUNK), 1)
    tri = (ti >= si).astype(BF16)
    rt = lax.broadcasted_iota(jnp.int32, (c2, 2 * c2), 0) % CHUNK
    cs = lax.broadcasted_iota(jnp.int32, (c2, 2 * c2), 1) % CHUNK
    strict = rt > cs
    incl = rt >= cs

    k_k, k_a, r_k = kk_ref[...], ka_ref[...], rk_ref[...]
    ln_w, ln_b = lw_ref[...], lb_ref[...]

    def chunk(c, carry):
        rows = pl.ds(pl.multiple_of(c * CHUNK, CHUNK), CHUNK)
        r, k, v, a = r_ref[rows, :], k_ref[rows, :], v_ref[rows, :], a_ref[rows, :]
        if has_vres:
            v = v + (vf_ref[rows, :] - v) * vg_ref[rows, :]
        lw = -DECAY_SCALE * jax.nn.sigmoid(w_ref[rows, :])
        lw_hi = lw.astype(BF16)
        lw_lo = (lw - lw_hi.astype(F32)).astype(BF16)
        lp = _dot(tri, lw_hi) + _dot(tri, lw_lo)
        lp_end = lp[CHUNK - 1:CHUNK, :]

        kk = k * k_k
        norm = jnp.sqrt(_dot_hl(kk * kk, head_ones))
        kk = kk / jnp.maximum(norm, 1e-12)
        kmod = k * (1.0 + (a - 1.0) * k_a)
        bb = kk * a

        en = jnp.exp(-lp)
        to_end = jnp.exp(lp_end - lp)
        r_t = r * jnp.exp(lp)
        a_t = -kk * jnp.exp(lp - lw)
        lhs = jnp.concatenate([stack(a_t), stack(r_t)], axis=0)
        rhs = jnp.concatenate([stack(bb * en), stack(kmod * en)], axis=0)
        bk_end = jnp.concatenate([stack(bb * to_end), stack(kmod * to_end)], axis=0)
        vs = stack(v)

        gram = lax.dot_general(lhs, rhs, _NT, preferred_element_type=F32)
        top = jnp.where(strict, gram[:c2, :], 0.0)
        bot = jnp.where(incl, gram[c2:, :], 0.0).astype(BF16)
        n_ab = top[:, :c2]
        n_ak = top[:, c2:].astype(BF16)

        inv = eye + n_ab
        pw = n_ab.astype(BF16)
        for _ in range(5):
            pw_f = _dot(pw, pw)
            pw = pw_f.astype(BF16)
            inv = inv + _dot(inv.astype(BF16), pw)

        ht = ht_ref[...]
        ht_b = ht.astype(BF16)
        state_proj = lax.dot_general(lhs, ht_b, _NT, preferred_element_type=F32)
        x = state_proj[:c2, :] + _dot(n_ak, vs)
        us = _dot(inv.astype(BF16), x.astype(BF16)).astype(BF16)
        uv = jnp.concatenate([us, vs], axis=0)
        ys = state_proj[c2:, :] + _dot(bot, uv)
        ht_ref[...] = ht * jnp.exp(lp_end) + lax.dot_general(
            uv, bk_end, _TN, preferred_element_type=F32)
        y = ys[:CHUNK, :] + ys[CHUNK:, :]

        mean = _dot_hl(y, head_ones) * (1.0 / HEAD)
        yc = y - mean
        var = _dot_hl(yc * yc, head_ones) * (1.0 / HEAD)
        out = yc * lax.rsqrt(var + GN_EPS) * ln_w + ln_b
        out = out + _dot_hl(r * kmod * r_k, head_ones) * v
        o_ref[rows, :] = (out * g_ref[rows, :]).astype(o_ref.dtype)
        return carry

    lax.fori_loop(0, n_chunks, chunk, 0)


def _rwkv_scan(r, k, v, w, a, g, k_k, k_a, r_k, ln_w, ln_b, seq_len, vres=None, *, tc=256):
    m, d = r.shape
    tc = _tile(seq_len, tc)
    nt = seq_len // tc
    row_spec = pl.BlockSpec((tc, LANES), lambda b, h, c: (b * nt + c, h))
    par_spec = pl.BlockSpec((1, LANES), lambda b, h, c: (0, h))
    seqs = [r, k, v, w, a, g] + (list(vres) if vres is not None else [])
    pars = [p.reshape(1, d) for p in (k_k, k_a, r_k, ln_w, ln_b)]
    return pl.pallas_call(
        functools.partial(_scan_kernel, has_vres=vres is not None, n_chunks=tc // CHUNK),
        out_shape=jax.ShapeDtypeStruct((m, d), BF16),
        grid=(m // seq_len, d // LANES, nt),
        in_specs=[row_spec] * len(seqs) + [par_spec] * len(pars),
        out_specs=row_spec,
        scratch_shapes=[pltpu.VMEM((LANES, LANES), F32)],
        compiler_params=_params(("parallel", "parallel", "arbitrary")),
        name="rwkv_scan",
    )(*seqs, *pars)


def _swa_kernel(sink_ref, q_ref, kp_ref, kc_ref, vp_ref, vc_ref, o_ref, *, n_kv, q_per_kv, blocks_per_seq):
    first = pl.program_id(0) % blocks_per_seq == 0
    lane = lax.broadcasted_iota(jnp.int32, (ATT_BLOCK, LANES), 1)
    even = lane < HEAD
    qi = lax.broadcasted_iota(jnp.int32, (ATT_BLOCK, 2 * ATT_BLOCK), 0)
    kc = lax.broadcasted_iota(jnp.int32, (ATT_BLOCK, 2 * ATT_BLOCK), 1)
    valid = (kc > qi) & (kc <= qi + ATT_BLOCK) & ((kc >= ATT_BLOCK) | jnp.logical_not(first))
    for gi in range(n_kv):
        cols = slice(gi * LANES, (gi + 1) * LANES)
        keys = jnp.concatenate([kp_ref[:, cols], kc_ref[:, cols]], axis=0)
        vals = jnp.concatenate([vp_ref[:, cols], vc_ref[:, cols]], axis=0)
        for pi in range(q_per_kv // 2):
            tile = gi * (q_per_kv // 2) + pi
            q2 = q_ref[:, tile * LANES:(tile + 1) * LANES]
            outs = []
            for half in range(2):
                sink = sink_ref[2 * tile + half]
                qh = jnp.where(even if half == 0 else jnp.logical_not(even), q2, jnp.zeros_like(q2))
                logits = lax.dot_general(qh, keys, _NT, preferred_element_type=F32) * (HEAD ** -0.5)
                logits = jnp.where(valid, logits, -jnp.inf)
                mx = jnp.maximum(jnp.max(logits, axis=-1, keepdims=True), sink)
                p = jnp.exp(logits - mx)
                denom = jnp.sum(p, axis=-1, keepdims=True) + jnp.exp(sink - mx)
                outs.append(_dot((p / denom).astype(BF16), vals))
            o_ref[:, tile * LANES:(tile + 1) * LANES] = jnp.where(even, outs[0], outs[1]).astype(o_ref.dtype)


def _swa(qkv, sinks, seq_len, d, n_kv):
    m = qkv.shape[0]
    n_q = d // HEAD
    q_tiles = d // LANES
    blocks_per_seq = seq_len // ATT_BLOCK
    kw = n_kv * LANES
    k_blk = d // kw
    prev = lambda i: jnp.maximum(i - 1, 0)
    return pl.pallas_call(
        functools.partial(_swa_kernel, n_kv=n_kv, q_per_kv=n_q // n_kv, blocks_per_seq=blocks_per_seq),
        out_shape=jax.ShapeDtypeStruct((m, d), BF16),
        grid=(m // ATT_BLOCK,),
        in_specs=[
            pl.BlockSpec(memory_space=pltpu.SMEM),
            pl.BlockSpec((ATT_BLOCK, d), lambda i: (i, 0)),
            pl.BlockSpec((ATT_BLOCK, kw), lambda i: (prev(i), k_blk)),
            pl.BlockSpec((ATT_BLOCK, kw), lambda i: (i, k_blk)),
            pl.BlockSpec((ATT_BLOCK, kw), lambda i: (prev(i), k_blk + 1)),
            pl.BlockSpec((ATT_BLOCK, kw), lambda i: (i, k_blk + 1)),
        ],
        out_specs=pl.BlockSpec((ATT_BLOCK, d), lambda i: (i, 0)),
        compiler_params=_params(("parallel",)),
        name="swa",
    )(sinks, qkv, qkv, qkv, qkv, qkv)


def _conv_kernel(x_ref, p_ref, g_ref, wb_ref, wc_ref, wh_ref, cw_ref, o_ref, xn_ref, *, tiles_per_seq):
    @pl.when(pl.program_id(1) == 0)
    def _():
        g = g_ref[...]
        xn_ref[:SUBLANES, :] = _rms(p_ref[...], g).astype(BF16)
        xn_ref[SUBLANES:, :] = _rms(x_ref[...], g).astype(BF16)

    xn = xn_ref[...]
    u = _dot(xn, wc_ref[...]) * _dot(xn, wh_ref[...])
    row = lax.broadcasted_iota(jnp.int32, u.shape, 0)
    first = pl.program_id(0) % tiles_per_seq == 0
    u = jnp.where((row < SUBLANES) & first, 0.0, u)
    cw = cw_ref[...]
    uc = (cw[2:3, :] * u + cw[1:2, :] * pltpu.roll(u, 1, 0) + cw[0:1, :] * pltpu.roll(u, 2, 0))[SUBLANES:, :]
    bg = _dot(xn[SUBLANES:, :], wb_ref[...])
    o_ref[...] = (bg * uc).astype(o_ref.dtype)


def _conv_front(x, g, w_in, conv_w, seq_len, *, tm=512, tn=512):
    m, d = x.shape
    tm, tn = _tile(seq_len, tm), _tile(d, tn)
    nj = d // tn
    blk = tm // SUBLANES
    return pl.pallas_call(
        functools.partial(_conv_kernel, tiles_per_seq=seq_len // tm),
        out_shape=jax.ShapeDtypeStruct((m, d), BF16),
        grid=(m // tm, nj),
        in_specs=[
            pl.BlockSpec((tm, d), lambda i, j: (i, 0)),
            pl.BlockSpec((SUBLANES, d), lambda i, j: (jnp.maximum(i * blk - 1, 0), 0)),
            pl.BlockSpec((1, d), lambda i, j: (0, 0)),
            pl.BlockSpec((d, tn), lambda i, j: (0, j)),
            pl.BlockSpec((d, tn), lambda i, j: (0, nj + j)),
            pl.BlockSpec((d, tn), lambda i, j: (0, 2 * nj + j)),
            pl.BlockSpec((conv_w.shape[0], tn), lambda i, j: (0, j)),
        ],
        out_specs=pl.BlockSpec((tm, tn), lambda i, j: (i, j)),
        scratch_shapes=[pltpu.VMEM((SUBLANES + tm, d), BF16)],
        compiler_params=_params(("parallel", "arbitrary")),
        name="conv_front",
    )(x, x, g.reshape(1, d), w_in, w_in, w_in, conv_w)


def _rwkv_layer(x, seq_len, g_norm, mu, w_rkv, w_o, w0, w1, w2, a0, a1, a2, g1, g2,
                k_k, k_a, r_k, ln_w, ln_b, v_first, vres):
    xr, xw, xk, xv, xa, xg = _rwkv_mix(x, g_norm, mu, seq_len)
    w_rkv = w_rkv.astype(BF16)
    r = _mm(xr, w_rkv[0], name="rwkv_r")
    k = _mm(xk, w_rkv[1], name="rwkv_k")
    v = _mm(xv, w_rkv[2], name="rwkv_v")
    w = _lora(xw, w1, w2, w0, mid_act="tanh")
    a = _lora(xa, a1, a2, a0, out_act="sigmoid")
    g = _lora(xg, g1, g2, mid_act="sigmoid")
    if vres is None:
        scan_vres = None
        v_first = v
    else:
        v0, v1, v2 = vres
        scan_vres = (v_first, _lora(xv, v1, v2, v0, out_act="sigmoid"))
    yg = _rwkv_scan(r, k, v, w, a, g, k_k, k_a, r_k.reshape(-1), ln_w, ln_b, seq_len, scan_vres)
    return _mm(yg, w_o.astype(BF16), res=x, name="rwkv_o"), v_first


def _swa_layer(x, seq_len, g_norm, w_qkv, b_qkv, w_o, b_o, sinks):
    d = x.shape[1]
    n_kv = (w_qkv.shape[1] - d) // (2 * HEAD)
    kvd = n_kv * HEAD

    def dup(t):
        lead = t.shape[:-1]
        t = t.reshape(lead + (n_kv, 1, HEAD))
        return jnp.broadcast_to(t, lead + (n_kv, 2, HEAD)).reshape(lead + (2 * kvd,))

    w_cat = jnp.concatenate([w_qkv[:, :d], dup(w_qkv[:, d:d + kvd]), dup(w_qkv[:, d + kvd:])], axis=1)
    b_cat = jnp.concatenate([b_qkv[:d], dup(b_qkv[d:d + kvd]), dup(b_qkv[d + kvd:])])
    qkv = _mm(x, w_cat.astype(BF16), norm_g=g_norm, bias=b_cat, out_dtype=BF16, name="swa_qkv")
    o = _swa(qkv, sinks, seq_len, d, n_kv)
    return _mm(o, w_o.astype(BF16), bias=b_o, res=x, name="swa_o")


def _conv_layer(x, seq_len, g_norm, w_in, conv_w, w_out):
    gated = _conv_front(x, g_norm, w_in.astype(BF16), conv_w, seq_len)
    return _mm(gated, w_out.astype(BF16), res=x, name="conv_o")


def kernel(x, norm_mix, norm_ffn, norm_final, rwkv_mu, rwkv_w_rkv, rwkv_w_o, rwkv_w0, rwkv_w1, rwkv_w2, rwkv_a0, rwkv_a1, rwkv_a2, rwkv_v0, rwkv_v1, rwkv_v2, rwkv_g1, rwkv_g2, rwkv_k_k, rwkv_k_a, rwkv_r_k, rwkv_lnx_w, rwkv_lnx_b, swa_w_qkv, swa_b_qkv, swa_w_o, swa_b_o, swa_sinks, conv_w_in, conv_w, conv_w_out, mlp_w_up, mlp_w_down):
    bsz, seq_len, d = x.shape
    depth = norm_mix.shape[0]
    xf = x.reshape(bsz * seq_len, d)
    v_first = None
    ia = ib = ic = 0
    for i in range(depth):
        kind = i % 3
        if kind == 0:
            vres = None if ia == 0 else (rwkv_v0[ia - 1], rwkv_v1[ia - 1], rwkv_v2[ia - 1])
            xf, v_first = _rwkv_layer(
                xf, seq_len, norm_mix[i], rwkv_mu[ia], rwkv_w_rkv[ia], rwkv_w_o[ia],
                rwkv_w0[ia], rwkv_w1[ia], rwkv_w2[ia], rwkv_a0[ia], rwkv_a1[ia], rwkv_a2[ia],
                rwkv_g1[ia], rwkv_g2[ia], rwkv_k_k[ia], rwkv_k_a[ia], rwkv_r_k[ia],
                rwkv_lnx_w[ia], rwkv_lnx_b[ia], v_first, vres)
            ia += 1
        elif kind == 1:
            xf = _swa_layer(xf, seq_len, norm_mix[i], swa_w_qkv[ib], swa_b_qkv[ib],
                            swa_w_o[ib], swa_b_o[ib], swa_sinks[ib])
            ib += 1
        else:
            xf = _conv_layer(xf, seq_len, norm_mix[i], conv_w_in[ic], conv_w[ic], conv_w_out[ic])
            ic += 1
        xf = _ffn(xf, norm_ffn[i], mlp_w_up[i].astype(BF16), mlp_w_down[i].astype(BF16),
                  norm_final if i == depth - 1 else None)
    return xf.reshape(bsz, seq_len, d)
```

```python
import functools

import jax
import jax.numpy as jnp
from jax import lax
from jax.experimental import pallas as pl
from jax.experimental.pallas import tpu as pltpu

F32 = jnp.float32
BF16 = jnp.bfloat16

HEAD = 64
LANES = 128
SUBLANES = 8
CHUNK = 64
ATT_BLOCK = 128
RMS_EPS = 1e-6
GN_EPS = 64e-5
DECAY_SCALE = 0.6065306597126334
VMEM_LIMIT = 56 * 1024 * 1024

_NT = (((1,), (1,)), ((), ()))
_TN = (((0,), (0,)), ((), ()))


def _tile(n, pref):
    t = min(n, pref)
    while n % t:
        t //= 2
    return t


def _params(sem):
    return pltpu.CompilerParams(dimension_semantics=sem, vmem_limit_bytes=VMEM_LIMIT)


def _rms(x, g):
    ms = jnp.mean(x * x, axis=-1, keepdims=True)
    return x * lax.rsqrt(ms + RMS_EPS) * g


def _dot(a, b):
    return jnp.dot(a, b, preferred_element_type=F32)


def _dot_hl(x, ones_bf16):
    hi = x.astype(BF16)
    lo = (x - hi.astype(F32)).astype(BF16)
    return _dot(hi, ones_bf16) + _dot(lo, ones_bf16)


def _mm_kernel(*refs, has_norm, has_bias, has_res):
    it = iter(refs)
    x_ref = next(it)
    g_ref = next(it) if has_norm else None
    w_ref = next(it)
    b_ref = next(it) if has_bias else None
    r_ref = next(it) if has_res else None
    o_ref = next(it)
    xn_ref = next(it) if has_norm else None

    if has_norm:
        @pl.when(pl.program_id(1) == 0)
        def _():
            xn_ref[...] = _rms(x_ref[...], g_ref[...]).astype(BF16)
        xb = xn_ref[...]
    else:
        xb = x_ref[...]
    acc = _dot(xb, w_ref[...])
    if has_bias:
        acc = acc + b_ref[...]
    if has_res:
        acc = acc + r_ref[...]
    o_ref[...] = acc.astype(o_ref.dtype)


def _mm(x, w, *, norm_g=None, bias=None, res=None, out_dtype=F32, tm=512, tn=512, name="mm"):
    m, k = x.shape
    n = w.shape[1]
    tm, tn = _tile(m, tm), _tile(n, tn)
    has_norm, has_bias, has_res = norm_g is not None, bias is not None, res is not None
    args, specs = [x], [pl.BlockSpec((tm, k), lambda i, j: (i, 0))]
    if has_norm:
        args.append(norm_g.reshape(1, k))
        specs.append(pl.BlockSpec((1, k), lambda i, j: (0, 0)))
    args.append(w)
    specs.append(pl.BlockSpec((k, tn), lambda i, j: (0, j)))
    if has_bias:
        args.append(bias.reshape(1, n))
        specs.append(pl.BlockSpec((1, tn), lambda i, j: (0, j)))
    if has_res:
        args.append(res)
        specs.append(pl.BlockSpec((tm, tn), lambda i, j: (i, j)))
    return pl.pallas_call(
        functools.partial(_mm_kernel, has_norm=has_norm, has_bias=has_bias, has_res=has_res),
        out_shape=jax.ShapeDtypeStruct((m, n), out_dtype),
        grid=(m // tm, n // tn),
        in_specs=specs,
        out_specs=pl.BlockSpec((tm, tn), lambda i, j: (i, j)),
        scratch_shapes=[pltpu.VMEM((tm, k), BF16)] if has_norm else [],
        compiler_params=_params(("parallel", "arbitrary")),
        name=name,
    )(*args)


def _ffn_kernel(*refs, has_final):
    if has_final:
        x_ref, g_ref, wu_ref, wd_ref, gf_ref, o_ref, xn_ref = refs
    else:
        x_ref, g_ref, wu_ref, wd_ref, o_ref, xn_ref = refs
    f = pl.program_id(1)

    @pl.when(f == 0)
    def _():
        x = x_ref[...]
        xn_ref[...] = _rms(x, g_ref[...]).astype(BF16)
        o_ref[...] = x

    h = jnp.maximum(_dot(xn_ref[...], wu_ref[...]), 0.0)
    o_ref[...] += _dot((h * h).astype(BF16), wd_ref[...])

    if has_final:
        @pl.when(f == pl.num_programs(1) - 1)
        def _():
            o_ref[...] = _rms(o_ref[...], gf_ref[...])


def _ffn(x, g, w_up, w_down, final_g=None, *, tm=512, tf=512):
    m, d = x.shape
    ff = w_up.shape[1]
    tm, tf = _tile(m, tm), _tile(ff, tf)
    has_final = final_g is not None
    args = [x, g.reshape(1, d), w_up, w_down]
    specs = [
        pl.BlockSpec((tm, d), lambda i, f: (i, 0)),
        pl.BlockSpec((1, d), lambda i, f: (0, 0)),
        pl.BlockSpec((d, tf), lambda i, f: (0, f)),
        pl.BlockSpec((tf, d), lambda i, f: (f, 0)),
    ]
    if has_final:
        args.append(final_g.reshape(1, d))
        specs.append(pl.BlockSpec((1, d), lambda i, f: (0, 0)))
    return pl.pallas_call(
        functools.partial(_ffn_kernel, has_final=has_final),
        out_shape=jax.ShapeDtypeStruct((m, d), F32),
        grid=(m // tm, ff // tf),
        in_specs=specs,
        out_specs=pl.BlockSpec((tm, d), lambda i, f: (i, 0)),
        scratch_shapes=[pltpu.VMEM((tm, d), BF16)],
        compiler_params=_params(("parallel", "arbitrary")),
        name="ffn",
    )(*args)


def _mix_kernel(x_ref, p_ref, g_ref, mu_ref, *o_refs, tiles_per_seq):
    g = g_ref[...]
    h = _rms(x_ref[...], g)
    hp = _rms(p_ref[...], g)[SUBLANES - 1:SUBLANES, :]
    hp = jnp.where(pl.program_id(0) % tiles_per_seq == 0, 0.0, hp)
    row = lax.broadcasted_iota(jnp.int32, h.shape, 0)
    shifted = jnp.where(row == 0, hp, pltpu.roll(h, 1, 0))
    xx = shifted - h
    for c, o_ref in enumerate(o_refs):
        o_ref[...] = (h + xx * mu_ref[c:c + 1, :]).astype(BF16)


def _rwkv_mix(x, g, mu, seq_len, *, tm=256):
    m, d = x.shape
    tm = _tile(seq_len, tm)
    ncoef = mu.shape[0]
    blk = tm // SUBLANES
    return pl.pallas_call(
        functools.partial(_mix_kernel, tiles_per_seq=seq_len // tm),
        out_shape=[jax.ShapeDtypeStruct((m, d), BF16)] * ncoef,
        grid=(m // tm,),
        in_specs=[
            pl.BlockSpec((tm, d), lambda i: (i, 0)),
            pl.BlockSpec((SUBLANES, d), lambda i: (jnp.maximum(i * blk - 1, 0), 0)),
            pl.BlockSpec((1, d), lambda i: (0, 0)),
            pl.BlockSpec((ncoef, d), lambda i: (0, 0)),
        ],
        out_specs=[pl.BlockSpec((tm, d), lambda i: (i, 0))] * ncoef,
        compiler_params=_params(("parallel",)),
        name="rwkv_mix",
    )(x, x, g.reshape(1, d), mu)


def _act(x, kind):
    if kind == "tanh":
        return jnp.tanh(x)
    if kind == "sigmoid":
        return jax.nn.sigmoid(x)
    return x


def _lora_kernel(*refs, mid_act, out_act, has_bias):
    if has_bias:
        x_ref, a_ref, b_ref, c_ref, o_ref = refs
    else:
        x_ref, a_ref, b_ref, o_ref = refs
    mid = _act(_dot(x_ref[...], a_ref[...]), mid_act).astype(BF16)
    out = _dot(mid, b_ref[...])
    if has_bias:
        out = out + c_ref[...]
    o_ref[...] = _act(out, out_act).astype(o_ref.dtype)


def _lora(x, a, b, bias=None, *, mid_act=None, out_act=None, tm=512):
    m, d = x.shape
    rank, n = b.shape
    pad = -rank % LANES
    if pad:
        a = jnp.pad(a, ((0, 0), (0, pad)))
        b = jnp.pad(b, ((0, pad), (0, 0)))
        rank += pad
    tm = _tile(m, tm)
    has_bias = bias is not None
    args = [x, a.astype(BF16), b.astype(BF16)]
    specs = [
        pl.BlockSpec((tm, d), lambda i: (i, 0)),
        pl.BlockSpec((d, rank), lambda i: (0, 0)),
        pl.BlockSpec((rank, n), lambda i: (0, 0)),
    ]
    if has_bias:
        args.append(bias.reshape(1, n))
        specs.append(pl.BlockSpec((1, n), lambda i: (0, 0)))
    return pl.pallas_call(
        functools.partial(_lora_kernel, mid_act=mid_act, out_act=out_act, has_bias=has_bias),
        out_shape=jax.ShapeDtypeStruct((m, n), F32),
        grid=(m // tm,),
        in_specs=specs,
        out_specs=pl.BlockSpec((tm, n), lambda i: (i, 0)),
        compiler_params=_params(("parallel",)),
        name="lora",
    )(*args)


def _scan_kernel(*refs, has_vres, n_chunks, n_pairs):
    if has_vres:
        (r_ref, k_ref, v_ref, w_ref, a_ref, g_ref, vf_ref, vg_ref,
         kk_ref, ka_ref, rk_ref, lw_ref, lb_ref, o_ref, ht_ref) = refs
    else:
        (r_ref, k_ref, v_ref, w_ref, a_ref, g_ref,
         kk_ref, ka_ref, rk_ref, lw_ref, lb_ref, o_ref, ht_ref) = refs
    c2 = 2 * CHUNK

    @pl.when(pl.program_id(2) == 0)
    def _():
        ht_ref[...] = jnp.zeros_like(ht_ref)

    lane = lax.broadcasted_iota(jnp.int32, (CHUNK, LANES), 1)
    head0 = lane < HEAD

    def stack(x):
        return jnp.concatenate([jnp.where(head0, x, 0.0), jnp.where(head0, 0.0, x)], axis=0).astype(BF16)

    ri = lax.broadcasted_iota(jnp.int32, (LANES, LANES), 0)
    ci = lax.broadcasted_iota(jnp.int32, (LANES, LANES), 1)
    head_ones = ((ri // HEAD) == (ci // HEAD)).astype(BF16)
    eye = (ri == ci).astype(F32)
    ti = lax.broadcasted_iota(jnp.int32, (CHUNK, CHUNK), 0)
    si = lax.broadcasted_iota(jnp.int32, (CHUNK, CHUNK), 1)
    tri = (ti >= si).astype(BF16)
    rt = lax.broadcasted_iota(jnp.int32, (c2, 2 * c2), 0) % CHUNK
    cs = lax.broadcasted_iota(jnp.int32, (c2, 2 * c2), 1) % CHUNK
    strict = rt > cs
    incl = rt >= cs

    def pair_chunk(rows, p):
        cols = slice(p * LANES, (p + 1) * LANES)
        r, k, v, a = r_ref[rows, cols], k_ref[rows, cols], v_ref[rows, cols], a_ref[rows, cols]
        k_k, k_a, r_k = kk_ref[:, cols], ka_ref[:, cols], rk_ref[:, cols]
        if has_vres:
            v = v + (vf_ref[rows, cols] - v) * vg_ref[rows, cols]
        lw = -DECAY_SCALE * jax.nn.sigmoid(w_ref[rows, cols])
        lw_hi = lw.astype(BF16)
        lw_lo = (lw - lw_hi.astype(F32)).astype(BF16)
        lp = _dot(tri, lw_hi) + _dot(tri, lw_lo)
        kk = k * k_k
        norm = jnp.sqrt(_dot_hl(kk * kk, head_ones))
        yield
        lp_end = lp[CHUNK - 1:CHUNK, :]
        kk = kk / jnp.maximum(norm, 1e-12)
        kmod = k * (1.0 + (a - 1.0) * k_a)
        bb = kk * a

        en = jnp.exp(-lp)
        to_end = jnp.exp(lp_end - lp)
        r_t = r * jnp.exp(lp)
        a_t = -kk * jnp.exp(lp - lw)
        lhs = jnp.concatenate([stack(a_t), stack(r_t)], axis=0)
        rhs = jnp.concatenate([stack(bb * en), stack(kmod * en)], axis=0)
        bk_end = jnp.concatenate([stack(bb * to_end), stack(kmod * to_end)], axis=0)
        vs = stack(v)

        gram = lax.dot_general(lhs, rhs, _NT, preferred_element_type=F32)
        yield
        top =jnp.where(strict, gram[:c2, :], 0.0)
        bot = jnp.where(incl, gram[c2:, :], 0.0).astype(BF16)
        n_ab = top[:, :c2]
        n_ak = top[:, c2:].astype(BF16)

        inv = eye + n_ab
        pw = n_ab.astype(BF16)
        for _ in range(5):
            pw = _dot(pw, pw).astype(BF16)
            yield
            inv = inv + _dot(inv.astype(BF16), pw)

        ht = ht_ref[p]
        state_proj = lax.dot_general(lhs, ht.astype(BF16), _NT, preferred_element_type=F32)
        x = state_proj[:c2, :] + _dot(n_ak, vs)
        yield
        us = _dot(inv.astype(BF16), x.astype(BF16)).astype(BF16)
        yield
        uv = jnp.concatenate([us, vs], axis=0)
        ys = state_proj[c2:, :] + _dot(bot, uv)
        ht_ref[p] = ht * jnp.exp(lp_end) + lax.dot_general(uv, bk_end, _TN, preferred_element_type=F32)
        yield
        y = ys[:CHUNK, :] + ys[CHUNK:, :]

        mean = _dot_hl(y, head_ones) * (1.0 / HEAD)
        yield
        yc = y - mean
        var = _dot_hl(yc * yc, head_ones) * (1.0 / HEAD)
        bonus = _dot_hl(r * kmod * r_k, head_ones)
        yield
        out = yc * lax.rsqrt(var + GN_EPS) * lw_ref[:, cols] + lb_ref[:, cols]
        out = out + bonus * v
        o_ref[rows, cols] = (out * g_ref[rows, cols]).astype(o_ref.dtype)
        yield

    def chunk(c, carry):
        rows = pl.ds(pl.multiple_of(c * CHUNK, CHUNK), CHUNK)
        for _ in zip(*[pair_chunk(rows, p) for p in range(n_pairs)]):
            pass
        return carry

    lax.fori_loop(0, n_chunks, chunk, 0)


def _rwkv_scan(r, k, v, w, a, g, k_k, k_a, r_k, ln_w, ln_b, seq_len, vres=None, *, tc=256, n_pairs=8):
    m, d = r.shape
    tc = _tile(seq_len, tc)
    nt = seq_len // tc
    n_pairs = _tile(d // LANES, n_pairs)
    width = n_pairs * LANES
    row_spec = pl.BlockSpec((tc, width), lambda b, h, c: (b * nt + c, h))
    par_spec = pl.BlockSpec((1, width), lambda b, h, c: (0, h))
    seqs = [r, k, v, w, a, g] + (list(vres) if vres is not None else [])
    pars = [p.reshape(1, d) for p in (k_k, k_a, r_k, ln_w, ln_b)]
    return pl.pallas_call(
        functools.partial(_scan_kernel, has_vres=vres is not None, n_chunks=tc // CHUNK, n_pairs=n_pairs),
        out_shape=jax.ShapeDtypeStruct((m, d), BF16),
        grid=(m // seq_len, d // width, nt),
        in_specs=[row_spec] * len(seqs) + [par_spec] * len(pars),
        out_specs=row_spec,
        scratch_shapes=[pltpu.VMEM((n_pairs, LANES, LANES), F32)],
        compiler_params=_params(("parallel", "parallel", "arbitrary")),
        name="rwkv_scan",
    )(*seqs, *pars)


def _swa_kernel(sink_ref, q_ref, kp_ref, kc_ref, vp_ref, vc_ref, o_ref, *, n_kv, q_per_kv, blocks_per_seq):
    first = pl.program_id(0) % blocks_per_seq == 0
    lane = lax.broadcasted_iota(jnp.int32, (ATT_BLOCK, LANES), 1)
    even = lane < HEAD
    qi = lax.broadcasted_iota(jnp.int32, (ATT_BLOCK, 2 * ATT_BLOCK), 0)
    kc = lax.broadcasted_iota(jnp.int32, (ATT_BLOCK, 2 * ATT_BLOCK), 1)
    valid = (kc > qi) & (kc <= qi + ATT_BLOCK) & ((kc >= ATT_BLOCK) | jnp.logical_not(first))
    for gi in range(n_kv):
        cols = slice(gi * LANES, (gi + 1) * LANES)
        keys = jnp.concatenate([kp_ref[:, cols], kc_ref[:, cols]], axis=0)
        vals = jnp.concatenate([vp_ref[:, cols], vc_ref[:, cols]], axis=0)
        for pi in range(q_per_kv // 2):
            tile = gi * (q_per_kv // 2) + pi
            q2 = q_ref[:, tile * LANES:(tile + 1) * LANES]
            outs = []
            for half in range(2):
                sink = sink_ref[2 * tile + half]
                qh = jnp.where(even if half == 0 else jnp.logical_not(even), q2, jnp.zeros_like(q2))
                logits = lax.dot_general(qh, keys, _NT, preferred_element_type=F32) * (HEAD ** -0.5)
                logits = jnp.where(valid, logits, -jnp.inf)
                mx = jnp.maximum(jnp.max(logits, axis=-1, keepdims=True), sink)
                p = jnp.exp(logits - mx)
                denom = jnp.sum(p, axis=-1, keepdims=True) + jnp.exp(sink - mx)
                outs.append(_dot((p / denom).astype(BF16), vals))
            o_ref[:, tile * LANES:(tile + 1) * LANES] = jnp.where(even, outs[0], outs[1]).astype(o_ref.dtype)


def _swa(qkv, sinks, seq_len, d, n_kv):
    m = qkv.shape[0]
    n_q = d // HEAD
    q_tiles = d // LANES
    blocks_per_seq = seq_len // ATT_BLOCK
    kw = n_kv * LANES
    k_blk = d // kw
    prev = lambda i: jnp.maximum(i - 1, 0)
    return pl.pallas_call(
        functools.partial(_swa_kernel, n_kv=n_kv, q_per_kv=n_q // n_kv, blocks_per_seq=blocks_per_seq),
        out_shape=jax.ShapeDtypeStruct((m, d), BF16),
        grid=(m // ATT_BLOCK,),
        in_specs=[
            pl.BlockSpec(memory_space=pltpu.SMEM),
            pl.BlockSpec((ATT_BLOCK, d), lambda i: (i, 0)),
            pl.BlockSpec((ATT_BLOCK, kw), lambda i: (prev(i), k_blk)),
            pl.BlockSpec((ATT_BLOCK, kw), lambda i: (i, k_blk)),
            pl.BlockSpec((ATT_BLOCK, kw), lambda i: (prev(i), k_blk + 1)),
            pl.BlockSpec((ATT_BLOCK, kw), lambda i: (i, k_blk + 1)),
        ],
        out_specs=pl.BlockSpec((ATT_BLOCK, d), lambda i: (i, 0)),
        compiler_params=_params(("parallel",)),
        name="swa",
    )(sinks, qkv, qkv, qkv, qkv, qkv)


def _conv_kernel(x_ref, p_ref, g_ref, wb_ref, wc_ref, wh_ref, cw_ref, o_ref, xn_ref, *, tiles_per_seq):
    @pl.when(pl.program_id(1) == 0)
    def _():
        g = g_ref[...]
        xn_ref[:SUBLANES, :] = _rms(p_ref[...], g).astype(BF16)
        xn_ref[SUBLANES:, :] = _rms(x_ref[...], g).astype(BF16)

    xn = xn_ref[...]
    u = _dot(xn, wc_ref[...]) * _dot(xn, wh_ref[...])
    row = lax.broadcasted_iota(jnp.int32, u.shape, 0)
    first = pl.program_id(0) % tiles_per_seq == 0
    u = jnp.where((row < SUBLANES) & first, 0.0, u)
    cw = cw_ref[...]
    uc = (cw[2:3, :] * u + cw[1:2, :] * pltpu.roll(u, 1, 0) + cw[0:1, :] * pltpu.roll(u, 2, 0))[SUBLANES:, :]
    bg = _dot(xn[SUBLANES:, :], wb_ref[...])
    o_ref[...] = (bg * uc).astype(o_ref.dtype)


def _conv_front(x, g, w_in, conv_w, seq_len, *, tm=512, tn=512):
    m, d = x.shape
    tm, tn = _tile(seq_len, tm), _tile(d, tn)
    nj = d // tn
    blk = tm // SUBLANES
    return pl.pallas_call(
        functools.partial(_conv_kernel, tiles_per_seq=seq_len // tm),
        out_shape=jax.ShapeDtypeStruct((m, d), BF16),
        grid=(m // tm, nj),
        in_specs=[
            pl.BlockSpec((tm, d), lambda i, j: (i, 0)),
            pl.BlockSpec((SUBLANES, d), lambda i, j: (jnp.maximum(i * blk - 1, 0), 0)),
            pl.BlockSpec((1, d), lambda i, j: (0, 0)),
            pl.BlockSpec((d, tn), lambda i, j: (0, j)),
            pl.BlockSpec((d, tn), lambda i, j: (0, nj + j)),
            pl.BlockSpec((d, tn), lambda i, j: (0, 2 * nj + j)),
            pl.BlockSpec((conv_w.shape[0], tn), lambda i, j: (0, j)),
        ],
        out_specs=pl.BlockSpec((tm, tn), lambda i, j: (i, j)),
        scratch_shapes=[pltpu.VMEM((SUBLANES + tm, d), BF16)],
        compiler_params=_params(("parallel", "arbitrary")),
        name="conv_front",
    )(x, x, g.reshape(1, d), w_in, w_in, w_in, conv_w)


def _rwkv_layer(x, seq_len, g_norm, mu, w_rkv, w_o, w0, w1, w2, a0, a1, a2, g1, g2,
                k_k, k_a, r_k, ln_w, ln_b, v_first, vres):
    xr, xw, xk, xv, xa, xg = _rwkv_mix(x, g_norm, mu, seq_len)
    w_rkv = w_rkv.astype(BF16)
    r = _mm(xr, w_rkv[0], name="rwkv_r")
    k = _mm(xk, w_rkv[1], name="rwkv_k")
    v = _mm(xv, w_rkv[2], name="rwkv_v")
    w = _lora(xw, w1, w2, w0, mid_act="tanh")
    a = _lora(xa, a1, a2, a0, out_act="sigmoid")
    g = _lora(xg, g1, g2, mid_act="sigmoid")
    if vres is None:
        scan_vres = None
        v_first = v
    else:
        v0, v1, v2 = vres
        scan_vres = (v_first, _lora(xv, v1, v2, v0, out_act="sigmoid"))
    yg = _rwkv_scan(r, k, v, w, a, g, k_k, k_a, r_k.reshape(-1), ln_w, ln_b, seq_len, scan_vres)
    return _mm(yg, w_o.astype(BF16), res=x, name="rwkv_o"), v_first


def _swa_layer(x, seq_len, g_norm, w_qkv, b_qkv, w_o, b_o, sinks):
    d = x.shape[1]
    n_kv = (w_qkv.shape[1] - d) // (2 * HEAD)
    kvd = n_kv * HEAD

    def dup(t):
        lead = t.shape[:-1]
        t = t.reshape(lead + (n_kv, 1, HEAD))
        return jnp.broadcast_to(t, lead + (n_kv, 2, HEAD)).reshape(lead + (2 * kvd,))

    w_cat = jnp.concatenate([w_qkv[:, :d], dup(w_qkv[:, d:d + kvd]), dup(w_qkv[:, d + kvd:])], axis=1)
    b_cat = jnp.concatenate([b_qkv[:d], dup(b_qkv[d:d + kvd]), dup(b_qkv[d + kvd:])])
    qkv = _mm(x, w_cat.astype(BF16), norm_g=g_norm, bias=b_cat, out_dtype=BF16, name="swa_qkv")
    o = _swa(qkv, sinks, seq_len, d, n_kv)
    return _mm(o, w_o.astype(BF16), bias=b_o, res=x, name="swa_o")


def _conv_layer(x, seq_len, g_norm, w_in, conv_w, w_out):
    gated = _conv_front(x, g_norm, w_in.astype(BF16), conv_w, seq_len)
    return _mm(gated, w_out.astype(BF16), res=x, name="conv_o")


def kernel(x, norm_mix, norm_ffn, norm_final, rwkv_mu, rwkv_w_rkv, rwkv_w_o, rwkv_w0, rwkv_w1, rwkv_w2, rwkv_a0, rwkv_a1, rwkv_a2, rwkv_v0, rwkv_v1, rwkv_v2, rwkv_g1, rwkv_g2, rwkv_k_k, rwkv_k_a, rwkv_r_k, rwkv_lnx_w, rwkv_lnx_b, swa_w_qkv, swa_b_qkv, swa_w_o, swa_b_o, swa_sinks, conv_w_in, conv_w, conv_w_out, mlp_w_up, mlp_w_down):
    bsz, seq_len, d = x.shape
    depth = norm_mix.shape[0]
    xf = x.reshape(bsz * seq_len, d)
    v_first = None
    ia = ib = ic = 0
    for i in range(depth):
        kind = i % 3
        if kind == 0:
            vres = None if ia == 0 else (rwkv_v0[ia - 1], rwkv_v1[ia - 1], rwkv_v2[ia - 1])
            xf, v_first = _rwkv_layer(
                xf, seq_len, norm_mix[i], rwkv_mu[ia], rwkv_w_rkv[ia], rwkv_w_o[ia],
                rwkv_w0[ia], rwkv_w1[ia], rwkv_w2[ia], rwkv_a0[ia], rwkv_a1[ia], rwkv_a2[ia],
                rwkv_g1[ia], rwkv_g2[ia], rwkv_k_k[ia], rwkv_k_a[ia], rwkv_r_k[ia],
                rwkv_lnx_w[ia], rwkv_lnx_b[ia], v_first, vres)
            ia += 1
        elif kind == 1:
            xf = _swa_layer(xf, seq_len, norm_mix[i], swa_w_qkv[ib], swa_b_qkv[ib],
                            swa_w_o[ib], swa_b_o[ib], swa_sinks[ib])
            ib += 1
        else:
            xf = _conv_layer(xf, seq_len, norm_mix[i], conv_w_in[ic], conv_w[ic], conv_w_out[ic])
            ic += 1
        xf = _ffn(xf, norm_ffn[i], mlp_w_up[i].astype(BF16), mlp_w_down[i].astype(BF16),
                  norm_final if i == depth - 1 else None)
    return xf.reshape(bsz, seq_len, d)
```

```python
import functools

import jax
import jax.numpy as jnp
from jax import lax
from jax.experimental import pallas as pl
from jax.experimental.pallas import tpu as pltpu

F32 = jnp.float32
BF16 = jnp.bfloat16

HEAD = 64
LANES = 128
SUBLANES = 8
CHUNK = 64
ATT_BLOCK = 128
RMS_EPS = 1e-6
GN_EPS = 64e-5
DECAY_SCALE = 0.6065306597126334
VMEM_LIMIT = 56 * 1024 * 1024

_NT = (((1,), (1,)), ((), ()))
_TN = (((0,), (0,)), ((), ()))


def _tile(n, pref):
    t = min(n, pref)
    while n % t:
        t //= 2
    return t


def _params(sem):
    return pltpu.CompilerParams(dimension_semantics=sem, vmem_limit_bytes=VMEM_LIMIT)


def _rms(x, g):
    ms = jnp.mean(x * x, axis=-1, keepdims=True)
    return x * lax.rsqrt(ms + RMS_EPS) * g


def _dot(a, b):
    return jnp.dot(a, b, preferred_element_type=F32)


def _split_hl(x):
    hi = x.astype(BF16)
    lo = (x - hi.astype(F32)).astype(BF16)
    return jnp.concatenate([hi, lo], axis=1)


def _mm_kernel(*refs, has_norm, has_bias, has_res):
    it = iter(refs)
    x_ref = next(it)
    g_ref = next(it) if has_norm else None
    w_ref = next(it)
    b_ref = next(it) if has_bias else None
    r_ref = next(it) if has_res else None
    o_ref = next(it)
    xn_ref = next(it) if has_norm else None

    if has_norm:
        @pl.when(pl.program_id(1) == 0)
        def _():
            xn_ref[...] = _rms(x_ref[...], g_ref[...]).astype(BF16)
        xb = xn_ref[...]
    else:
        xb = x_ref[...]
    acc = _dot(xb, w_ref[...])
    if has_bias:
        acc = acc + b_ref[...]
    if has_res:
        acc = acc + r_ref[...]
    o_ref[...] = acc.astype(o_ref.dtype)


def _mm(x, w, *, norm_g=None, bias=None, res=None, out_dtype=F32, tm=512, tn=4096, name="mm"):
    m, k = x.shape
    n = w.shape[1]
    tm, tn = _tile(m, tm), _tile(n, tn)
    has_norm, has_bias, has_res = norm_g is not None, bias is not None, res is not None
    args, specs = [x], [pl.BlockSpec((tm, k), lambda i, j: (i, 0))]
    if has_norm:
        args.append(norm_g.reshape(1, k))
        specs.append(pl.BlockSpec((1, k), lambda i, j: (0, 0)))
    args.append(w)
    specs.append(pl.BlockSpec((k, tn), lambda i, j: (0, j)))
    if has_bias:
        args.append(bias.reshape(1, n))
        specs.append(pl.BlockSpec((1, tn), lambda i, j: (0, j)))
    if has_res:
        args.append(res)
        specs.append(pl.BlockSpec((tm, tn), lambda i, j: (i, j)))
    return pl.pallas_call(
        functools.partial(_mm_kernel, has_norm=has_norm, has_bias=has_bias, has_res=has_res),
        out_shape=jax.ShapeDtypeStruct((m, n), out_dtype),
        grid=(m // tm, n // tn),
        in_specs=specs,
        out_specs=pl.BlockSpec((tm, tn), lambda i, j: (i, j)),
        scratch_shapes=[pltpu.VMEM((tm, k), BF16)] if has_norm else [],
        compiler_params=_params(("parallel", "arbitrary")),
        name=name,
    )(*args)


def _ffn_kernel(*refs, has_final):
    if has_final:
        x_ref, g_ref, wu_ref, wd_ref, gf_ref, o_ref, xn_ref = refs
    else:
        x_ref, g_ref, wu_ref, wd_ref, o_ref, xn_ref = refs
    f = pl.program_id(1)

    @pl.when(f == 0)
    def _():
        x = x_ref[...]
        xn_ref[...] = _rms(x, g_ref[...]).astype(BF16)
        o_ref[...] = x

    h = jnp.maximum(_dot(xn_ref[...], wu_ref[...]), 0.0)
    o_ref[...] += _dot((h * h).astype(BF16), wd_ref[...])

    if has_final:
        @pl.when(f == pl.num_programs(1) - 1)
        def _():
            o_ref[...] = _rms(o_ref[...], gf_ref[...])


def _ffn(x, g, w_up, w_down, final_g=None, *, tm=512, tf=1024):
    m, d = x.shape
    ff = w_up.shape[1]
    tm, tf = _tile(m, tm), _tile(ff, tf)
    has_final = final_g is not None
    args = [x, g.reshape(1, d), w_up, w_down]
    specs = [
        pl.BlockSpec((tm, d), lambda i, f: (i, 0)),
        pl.BlockSpec((1, d), lambda i, f: (0, 0)),
        pl.BlockSpec((d, tf), lambda i, f: (0, f)),
        pl.BlockSpec((tf, d), lambda i, f: (f, 0)),
    ]
    if has_final:
        args.append(final_g.reshape(1, d))
        specs.append(pl.BlockSpec((1, d), lambda i, f: (0, 0)))
    return pl.pallas_call(
        functools.partial(_ffn_kernel, has_final=has_final),
        out_shape=jax.ShapeDtypeStruct((m, d), F32),
        grid=(m // tm, ff // tf),
        in_specs=specs,
        out_specs=pl.BlockSpec((tm, d), lambda i, f: (i, 0)),
        scratch_shapes=[pltpu.VMEM((tm, d), BF16)],
        compiler_params=_params(("parallel", "arbitrary")),
        name="ffn",
    )(*args)


def _mix_kernel(x_ref, p_ref, g_ref, mu_ref, *o_refs, tiles_per_seq):
    g = g_ref[...]
    h = _rms(x_ref[...], g)
    hp = _rms(p_ref[...], g)[SUBLANES - 1:SUBLANES, :]
    hp = jnp.where(pl.program_id(0) % tiles_per_seq == 0, 0.0, hp)
    row = lax.broadcasted_iota(jnp.int32, h.shape, 0)
    shifted = jnp.where(row == 0, hp, pltpu.roll(h, 1, 0))
    xx = shifted - h
    for c, o_ref in enumerate(o_refs):
        o_ref[...] = (h + xx * mu_ref[c:c + 1, :]).astype(BF16)


def _rwkv_mix(x, g, mu, seq_len, *, tm=256):
    m, d = x.shape
    tm = _tile(seq_len, tm)
    ncoef = mu.shape[0]
    blk = tm // SUBLANES
    return pl.pallas_call(
        functools.partial(_mix_kernel, tiles_per_seq=seq_len // tm),
        out_shape=[jax.ShapeDtypeStruct((m, d), BF16)] * ncoef,
        grid=(m // tm,),
        in_specs=[
            pl.BlockSpec((tm, d), lambda i: (i, 0)),
            pl.BlockSpec((SUBLANES, d), lambda i: (jnp.maximum(i * blk - 1, 0), 0)),
            pl.BlockSpec((1, d), lambda i: (0, 0)),
            pl.BlockSpec((ncoef, d), lambda i: (0, 0)),
        ],
        out_specs=[pl.BlockSpec((tm, d), lambda i: (i, 0))] * ncoef,
        compiler_params=_params(("parallel",)),
        name="rwkv_mix",
    )(x, x, g.reshape(1, d), mu)


def _act(x, kind):
    if kind == "tanh":
        return jnp.tanh(x)
    if kind == "sigmoid":
        return jax.nn.sigmoid(x)
    return x


def _lora_kernel(*refs, mid_act, out_act, has_bias):
    if has_bias:
        x_ref, a_ref, b_ref, c_ref, o_ref = refs
    else:
        x_ref, a_ref, b_ref, o_ref = refs
    mid = _act(_dot(x_ref[...], a_ref[...]), mid_act).astype(BF16)
    out = _dot(mid, b_ref[...])
    if has_bias:
        out = out + c_ref[...]
    o_ref[...] = _act(out, out_act).astype(o_ref.dtype)


def _lora(x, a, b, bias=None, *, mid_act=None, out_act=None, tm=512):
    m, d = x.shape
    rank, n = b.shape
    pad = -rank % LANES
    if pad:
        a = jnp.pad(a, ((0, 0), (0, pad)))
        b = jnp.pad(b, ((0, pad), (0, 0)))
        rank += pad
    tm = _tile(m, tm)
    has_bias = bias is not None
    args = [x, a.astype(BF16), b.astype(BF16)]
    specs = [
        pl.BlockSpec((tm, d), lambda i: (i, 0)),
        pl.BlockSpec((d, rank), lambda i: (0, 0)),
        pl.BlockSpec((rank, n), lambda i: (0, 0)),
    ]
    if has_bias:
        args.append(bias.reshape(1, n))
        specs.append(pl.BlockSpec((1, n), lambda i: (0, 0)))
    return pl.pallas_call(
        functools.partial(_lora_kernel, mid_act=mid_act, out_act=out_act, has_bias=has_bias),
        out_shape=jax.ShapeDtypeStruct((m, n), F32),
        grid=(m // tm,),
        in_specs=specs,
        out_specs=pl.BlockSpec((tm, n), lambda i: (i, 0)),
        compiler_params=_params(("parallel",)),
        name="lora",
    )(*args)


def _scan_kernel(*refs, has_vres, n_chunks, n_pairs):
    if has_vres:
        (r_ref, k_ref, v_ref, w_ref, a_ref, g_ref, vf_ref, vg_ref,
         kk_ref, ka_ref, rk_ref, lw_ref, lb_ref, o_ref, ht_ref) = refs
    else:
        (r_ref, k_ref, v_ref, w_ref, a_ref, g_ref,
         kk_ref, ka_ref, rk_ref, lw_ref, lb_ref, o_ref, ht_ref) = refs
    c2 = 2 * CHUNK

    @pl.when(pl.program_id(2) == 0)
    def _():
        ht_ref[...] = jnp.zeros_like(ht_ref)

    lane = lax.broadcasted_iota(jnp.int32, (CHUNK, LANES), 1)
    head0 = lane < HEAD

    def stack(x):
        return jnp.concatenate([jnp.where(head0, x, 0.0), jnp.where(head0, 0.0, x)], axis=0).astype(BF16)

    ri = lax.broadcasted_iota(jnp.int32, (LANES, LANES), 0)
    ci = lax.broadcasted_iota(jnp.int32, (LANES, LANES), 1)
    ri2 = lax.broadcasted_iota(jnp.int32, (2 * LANES, LANES), 0) % LANES
    ci2 = lax.broadcasted_iota(jnp.int32, (2 * LANES, LANES), 1)
    head_ones = ((ri2 // HEAD) == (ci2 // HEAD)).astype(BF16)

    def head_sum(x):
        return _dot(_split_hl(x), head_ones)
    eye = (ri == ci).astype(F32)
    ti = lax.broadcasted_iota(jnp.int32, (CHUNK, CHUNK), 0)
    si = lax.broadcasted_iota(jnp.int32, (CHUNK, CHUNK), 1)
    tri = (ti >= si).astype(BF16)
    rt = lax.broadcasted_iota(jnp.int32, (c2, 2 * c2), 0) % CHUNK
    cs = lax.broadcasted_iota(jnp.int32, (c2, 2 * c2), 1) % CHUNK
    strict = rt > cs
    incl = rt >= cs

    def pair_chunk(rows, p):
        cols = slice(p * LANES, (p + 1) * LANES)
        r, k, v, a = r_ref[rows, cols], k_ref[rows, cols], v_ref[rows, cols], a_ref[rows, cols]
        k_k, k_a, r_k = kk_ref[:, cols], ka_ref[:, cols], rk_ref[:, cols]
        if has_vres:
            v = v + (vf_ref[rows, cols] - v) * vg_ref[rows, cols]
        lw = -DECAY_SCALE * jax.nn.sigmoid(w_ref[rows, cols])
        lp = _dot(tri, _split_hl(lw))
        lp = lp[:, :LANES] + lp[:, LANES:]
        kk = k * k_k
        norm = jnp.sqrt(head_sum(kk * kk))
        yield
        lp_end = lp[CHUNK - 1:CHUNK, :]
        kk = kk / jnp.maximum(norm, 1e-12)
        kmod = k * (1.0 + (a - 1.0) * k_a)
        bb = kk * a

        en = jnp.exp(-lp)
        to_end = jnp.exp(lp_end - lp)
        r_t = r * jnp.exp(lp)
        a_t = -kk * jnp.exp(lp - lw)
        lhs = jnp.concatenate([stack(a_t), stack(r_t)], axis=0)
        rhs = jnp.concatenate([stack(bb * en), stack(kmod * en)], axis=0)
        bk_end = jnp.concatenate([stack(bb * to_end), stack(kmod * to_end)], axis=0)
        vs = stack(v)

        gram = lax.dot_general(lhs, rhs, _NT, preferred_element_type=F32)
        yield
        top =jnp.where(strict, gram[:c2, :], 0.0)
        bot = jnp.where(incl, gram[c2:, :], 0.0).astype(BF16)
        n_ab = top[:, :c2]
        n_ak = top[:, c2:].astype(BF16)

        inv = eye + n_ab
        pw = n_ab.astype(BF16)
        pw = _dot(pw, pw).astype(BF16)
        yield
        for _ in range(4):
            prod = _dot(pw, jnp.concatenate([pw, inv.astype(BF16)], axis=1))
            yield
            pw = prod[:, :c2].astype(BF16)
            inv = inv + prod[:, c2:]
        inv = inv + _dot(pw, inv.astype(BF16))

        ht = ht_ref[p]
        state_proj = lax.dot_general(lhs, ht.astype(BF16), _NT, preferred_element_type=F32)
        x = state_proj[:c2, :] + _dot(n_ak, vs)
        yield
        us = _dot(inv.astype(BF16), x.astype(BF16)).astype(BF16)
        yield
        uv = jnp.concatenate([us, vs], axis=0)
        ys = state_proj[c2:, :] + _dot(bot, uv)
        ht_ref[p] = ht * jnp.exp(lp_end) + lax.dot_general(uv, bk_end, _TN, preferred_element_type=F32)
        yield
        y = ys[:CHUNK, :] + ys[CHUNK:, :]

        mean = head_sum(y) * (1.0 / HEAD)
        yield
        yc = y - mean
        var = head_sum(yc * yc) * (1.0 / HEAD)
        bonus = head_sum(r * kmod * r_k)
        yield
        out = yc * lax.rsqrt(var + GN_EPS) * lw_ref[:, cols] + lb_ref[:, cols]
        out = out + bonus * v
        o_ref[rows, cols] = (out * g_ref[rows, cols]).astype(o_ref.dtype)
        yield

    def chunk(c, carry):
        rows = pl.ds(pl.multiple_of(c * CHUNK, CHUNK), CHUNK)
        for _ in zip(*[pair_chunk(rows, p) for p in range(n_pairs)]):
            pass
        return carry

    lax.fori_loop(0, n_chunks, chunk, 0)


def _rwkv_scan(r, k, v, w, a, g, k_k, k_a, r_k, ln_w, ln_b, seq_len, vres=None, *, tc=256, n_pairs=8):
    m, d = r.shape
    tc = _tile(seq_len, tc)
    nt = seq_len // tc
    n_pairs = _tile(d // LANES, n_pairs)
    width = n_pairs * LANES
    row_spec = pl.BlockSpec((tc, width), lambda b, h, c: (b * nt + c, h))
    par_spec = pl.BlockSpec((1, width), lambda b, h, c: (0, h))
    seqs = [r, k, v, w, a, g] + (list(vres) if vres is not None else [])
    pars = [p.reshape(1, d) for p in (k_k, k_a, r_k, ln_w, ln_b)]
    return pl.pallas_call(
        functools.partial(_scan_kernel, has_vres=vres is not None, n_chunks=tc // CHUNK, n_pairs=n_pairs),
        out_shape=jax.ShapeDtypeStruct((m, d), BF16),
        grid=(m // seq_len, d // width, nt),
        in_specs=[row_spec] * len(seqs) + [par_spec] * len(pars),
        out_specs=row_spec,
        scratch_shapes=[pltpu.VMEM((n_pairs, LANES, LANES), F32)],
        compiler_params=_params(("parallel", "parallel", "arbitrary")),
        name="rwkv_scan",
    )(*seqs, *pars)


def _swa_kernel(sink_ref, q_ref, kp_ref, kc_ref, vp_ref, vc_ref, o_ref, *, n_kv, q_per_kv, blocks_per_seq):
    first = pl.program_id(0) % blocks_per_seq == 0
    lane = lax.broadcasted_iota(jnp.int32, (ATT_BLOCK, LANES), 1)
    even = lane < HEAD
    qi = lax.broadcasted_iota(jnp.int32, (ATT_BLOCK, 2 * ATT_BLOCK), 0)
    kc = lax.broadcasted_iota(jnp.int32, (ATT_BLOCK, 2 * ATT_BLOCK), 1)
    valid = (kc > qi) & (kc <= qi + ATT_BLOCK) & ((kc >= ATT_BLOCK) | jnp.logical_not(first))
    for gi in range(n_kv):
        cols = slice(gi * LANES, (gi + 1) * LANES)
        keys = jnp.concatenate([kp_ref[:, cols], kc_ref[:, cols]], axis=0)
        vals = jnp.concatenate([vp_ref[:, cols], vc_ref[:, cols]], axis=0)
        for pi in range(q_per_kv // 2):
            tile = gi * (q_per_kv // 2) + pi
            q2 = q_ref[:, tile * LANES:(tile + 1) * LANES]
            outs = []
            for half in range(2):
                sink = sink_ref[2 * tile + half]
                qh = jnp.where(even if half == 0 else jnp.logical_not(even), q2, jnp.zeros_like(q2))
                logits = lax.dot_general(qh, keys, _NT, preferred_element_type=F32) * (HEAD ** -0.5)
                logits = jnp.where(valid, logits, -jnp.inf)
                mx = jnp.maximum(jnp.max(logits, axis=-1, keepdims=True), sink)
                p = jnp.exp(logits - mx)
                denom = jnp.sum(p, axis=-1, keepdims=True) + jnp.exp(sink - mx)
                outs.append(_dot((p / denom).astype(BF16), vals))
            o_ref[:, tile * LANES:(tile + 1) * LANES] = jnp.where(even, outs[0], outs[1]).astype(o_ref.dtype)


def _swa(qkv, sinks, seq_len, d, n_kv):
    m = qkv.shape[0]
    n_q = d // HEAD
    q_tiles = d // LANES
    blocks_per_seq = seq_len // ATT_BLOCK
    kw = n_kv * LANES
    k_blk = d // kw
    prev = lambda i: jnp.maximum(i - 1, 0)
    return pl.pallas_call(
        functools.partial(_swa_kernel, n_kv=n_kv, q_per_kv=n_q // n_kv, blocks_per_seq=blocks_per_seq),
        out_shape=jax.ShapeDtypeStruct((m, d), BF16),
        grid=(m // ATT_BLOCK,),
        in_specs=[
            pl.BlockSpec(memory_space=pltpu.SMEM),
            pl.BlockSpec((ATT_BLOCK, d), lambda i: (i, 0)),
            pl.BlockSpec((ATT_BLOCK, kw), lambda i: (prev(i), k_blk)),
            pl.BlockSpec((ATT_BLOCK, kw), lambda i: (i, k_blk)),
            pl.BlockSpec((ATT_BLOCK, kw), lambda i: (prev(i), k_blk + 1)),
            pl.BlockSpec((ATT_BLOCK, kw), lambda i: (i, k_blk + 1)),
        ],
        out_specs=pl.BlockSpec((ATT_BLOCK, d), lambda i: (i, 0)),
        compiler_params=_params(("parallel",)),
        name="swa",
    )(sinks, qkv, qkv, qkv, qkv, qkv)


def _conv_kernel(x_ref, p_ref, g_ref, wb_ref, wc_ref, wh_ref, cw_ref, o_ref, xn_ref, *, tiles_per_seq):
    @pl.when(pl.program_id(1) == 0)
    def _():
        g = g_ref[...]
        xn_ref[:SUBLANES, :] = _rms(p_ref[...], g).astype(BF16)
        xn_ref[SUBLANES:, :] = _rms(x_ref[...], g).astype(BF16)

    xn = xn_ref[...]
    u = _dot(xn, wc_ref[...]) * _dot(xn, wh_ref[...])
    row = lax.broadcasted_iota(jnp.int32, u.shape, 0)
    first = pl.program_id(0) % tiles_per_seq == 0
    u = jnp.where((row < SUBLANES) & first, 0.0, u)
    cw = cw_ref[...]
    uc = (cw[2:3, :] * u + cw[1:2, :] * pltpu.roll(u, 1, 0) + cw[0:1, :] * pltpu.roll(u, 2, 0))[SUBLANES:, :]
    bg = _dot(xn[SUBLANES:, :], wb_ref[...])
    o_ref[...] = (bg * uc).astype(o_ref.dtype)


def _conv_front(x, g, w_in, conv_w, seq_len, *, tm=512, tn=512):
    m, d = x.shape
    tm, tn = _tile(seq_len, tm), _tile(d, tn)
    nj = d // tn
    blk = tm // SUBLANES
    return pl.pallas_call(
        functools.partial(_conv_kernel, tiles_per_seq=seq_len // tm),
        out_shape=jax.ShapeDtypeStruct((m, d), BF16),
        grid=(m // tm, nj),
        in_specs=[
            pl.BlockSpec((tm, d), lambda i, j: (i, 0)),
            pl.BlockSpec((SUBLANES, d), lambda i, j: (jnp.maximum(i * blk - 1, 0), 0)),
            pl.BlockSpec((1, d), lambda i, j: (0, 0)),
            pl.BlockSpec((d, tn), lambda i, j: (0, j)),
            pl.BlockSpec((d, tn), lambda i, j: (0, nj + j)),
            pl.BlockSpec((d, tn), lambda i, j: (0, 2 * nj + j)),
            pl.BlockSpec((conv_w.shape[0], tn), lambda i, j: (0, j)),
        ],
        out_specs=pl.BlockSpec((tm, tn), lambda i, j: (i, j)),
        scratch_shapes=[pltpu.VMEM((SUBLANES + tm, d), BF16)],
        compiler_params=_params(("parallel", "arbitrary")),
        name="conv_front",
    )(x, x, g.reshape(1, d), w_in, w_in, w_in, conv_w)


def _rwkv_layer(x, seq_len, g_norm, mu, w_rkv, w_o, w0, w1, w2, a0, a1, a2, g1, g2,
                k_k, k_a, r_k, ln_w, ln_b, v_first, vres):
    xr, xw, xk, xv, xa, xg = _rwkv_mix(x, g_norm, mu, seq_len)
    w_rkv = w_rkv.astype(BF16)
    r = _mm(xr, w_rkv[0], name="rwkv_r")
    k = _mm(xk, w_rkv[1], name="rwkv_k")
    v = _mm(xv, w_rkv[2], name="rwkv_v")
    w = _lora(xw, w1, w2, w0, mid_act="tanh")
    a = _lora(xa, a1, a2, a0, out_act="sigmoid")
    g = _lora(xg, g1, g2, mid_act="sigmoid")
    if vres is None:
        scan_vres = None
        v_first = v
    else:
        v0, v1, v2 = vres
        scan_vres = (v_first, _lora(xv, v1, v2, v0, out_act="sigmoid"))
    yg = _rwkv_scan(r, k, v, w, a, g, k_k, k_a, r_k.reshape(-1), ln_w, ln_b, seq_len, scan_vres)
    return _mm(yg, w_o.astype(BF16), res=x, name="rwkv_o"), v_first


def _swa_layer(x, seq_len, g_norm, w_qkv, b_qkv, w_o, b_o, sinks):
    d = x.shape[1]
    n_kv = (w_qkv.shape[1] - d) // (2 * HEAD)
    kvd = n_kv * HEAD

    def dup(t):
        lead = t.shape[:-1]
        t = t.reshape(lead + (n_kv, 1, HEAD))
        return jnp.broadcast_to(t, lead + (n_kv, 2, HEAD)).reshape(lead + (2 * kvd,))

    w_cat = jnp.concatenate([w_qkv[:, :d], dup(w_qkv[:, d:d + kvd]), dup(w_qkv[:, d + kvd:])], axis=1)
    b_cat = jnp.concatenate([b_qkv[:d], dup(b_qkv[d:d + kvd]), dup(b_qkv[d + kvd:])])
    qkv = _mm(x, w_cat.astype(BF16), norm_g=g_norm, bias=b_cat, out_dtype=BF16, name="swa_qkv")
    o = _swa(qkv, sinks, seq_len, d, n_kv)
    return _mm(o, w_o.astype(BF16), bias=b_o, res=x, name="swa_o")


def _conv_layer(x, seq_len, g_norm, w_in, conv_w, w_out):
    gated = _conv_front(x, g_norm, w_in.astype(BF16), conv_w, seq_len)
    return _mm(gated, w_out.astype(BF16), res=x, name="conv_o")


def kernel(x, norm_mix, norm_ffn, norm_final, rwkv_mu, rwkv_w_rkv, rwkv_w_o, rwkv_w0, rwkv_w1, rwkv_w2, rwkv_a0, rwkv_a1, rwkv_a2, rwkv_v0, rwkv_v1, rwkv_v2, rwkv_g1, rwkv_g2, rwkv_k_k, rwkv_k_a, rwkv_r_k, rwkv_lnx_w, rwkv_lnx_b, swa_w_qkv, swa_b_qkv, swa_w_o, swa_b_o, swa_sinks, conv_w_in, conv_w, conv_w_out, mlp_w_up, mlp_w_down):
    bsz, seq_len, d = x.shape
    depth = norm_mix.shape[0]
    xf = x.reshape(bsz * seq_len, d)
    v_first = None
    ia = ib = ic = 0
    for i in range(depth):
        kind = i % 3
        if kind == 0:
            vres = None if ia == 0 else (rwkv_v0[ia - 1], rwkv_v1[ia - 1], rwkv_v2[ia - 1])
            xf, v_first = _rwkv_layer(
                xf, seq_len, norm_mix[i], rwkv_mu[ia], rwkv_w_rkv[ia], rwkv_w_o[ia],
                rwkv_w0[ia], rwkv_w1[ia], rwkv_w2[ia], rwkv_a0[ia], rwkv_a1[ia], rwkv_a2[ia],
                rwkv_g1[ia], rwkv_g2[ia], rwkv_k_k[ia], rwkv_k_a[ia], rwkv_r_k[ia],
                rwkv_lnx_w[ia], rwkv_lnx_b[ia], v_first, vres)
            ia += 1
        elif kind == 1:
            xf = _swa_layer(xf, seq_len, norm_mix[i], swa_w_qkv[ib], swa_b_qkv[ib],
                            swa_w_o[ib], swa_b_o[ib], swa_sinks[ib])
            ib += 1
        else:
            xf = _conv_layer(xf, seq_len, norm_mix[i], conv_w_in[ic], conv_w[ic], conv_w_out[ic])
            ic += 1
        xf = _ffn(xf, norm_ffn[i], mlp_w_up[i].astype(BF16), mlp_w_down[i].astype(BF16),
                  norm_final if i == depth - 1 else None)
    return xf.reshape(bsz, seq_len, d)
```

```python
import functools

import jax
import jax.numpy as jnp
from jax import lax
from jax.experimental import pallas as pl
from jax.experimental.pallas import tpu as pltpu

F32 = jnp.float32
BF16 = jnp.bfloat16

HEAD = 64
LANES = 128
SUBLANES = 8
CHUNK = 64
ATT_BLOCK = 128
RMS_EPS = 1e-6
GN_EPS = 64e-5
DECAY_SCALE = 0.6065306597126334
VMEM_LIMIT = 56 * 1024 * 1024

_NT = (((1,), (1,)), ((), ()))
_TN = (((0,), (0,)), ((), ()))


def _tile(n, pref):
    t = min(n, pref)
    while n % t:
        t //= 2
    return t


def _params(sem):
    return pltpu.CompilerParams(dimension_semantics=sem, vmem_limit_bytes=VMEM_LIMIT)


def _rms(x, g):
    ms = jnp.mean(x * x, axis=-1, keepdims=True)
    return x * lax.rsqrt(ms + RMS_EPS) * g


def _dot(a, b):
    return jnp.dot(a, b, preferred_element_type=F32)


def _split_hl(x):
    hi = x.astype(BF16)
    lo = (x - hi.astype(F32)).astype(BF16)
    return jnp.concatenate([hi, lo], axis=1)


def _mm_kernel(*refs, has_norm, has_bias, has_res):
    it = iter(refs)
    x_ref = next(it)
    g_ref = next(it) if has_norm else None
    w_ref = next(it)
    b_ref = next(it) if has_bias else None
    r_ref = next(it) if has_res else None
    o_ref = next(it)
    xn_ref = next(it) if has_norm else None

    if has_norm:
        @pl.when(pl.program_id(1) == 0)
        def _():
            xn_ref[...] = _rms(x_ref[...], g_ref[...]).astype(BF16)
        xb = xn_ref[...]
    else:
        xb = x_ref[...]
    acc = _dot(xb, w_ref[...])
    if has_bias:
        acc = acc + b_ref[...]
    if has_res:
        acc = acc + r_ref[...]
    o_ref[...] = acc.astype(o_ref.dtype)


def _mm(x, w, *, w_lead=(), norm_g=None, bias=None, res=None, out_dtype=F32, tm=512, tn=4096, name="mm"):
    m, k = x.shape
    n = w.shape[-1]
    squeezed = (None,) * len(w_lead)
    tm, tn = _tile(m, tm), _tile(n, tn)
    has_norm, has_bias, has_res = norm_g is not None, bias is not None, res is not None
    args, specs = [x], [pl.BlockSpec((tm, k), lambda i, j: (i, 0))]
    if has_norm:
        args.append(norm_g.reshape(1, k))
        specs.append(pl.BlockSpec((1, k), lambda i, j: (0, 0)))
    args.append(w)
    specs.append(pl.BlockSpec(squeezed + (k, tn), lambda i, j: w_lead + (0, j)))
    if has_bias:
        args.append(bias.reshape(1, n))
        specs.append(pl.BlockSpec((1, tn), lambda i, j: (0, j)))
    if has_res:
        args.append(res)
        specs.append(pl.BlockSpec((tm, tn), lambda i, j: (i, j)))
    return pl.pallas_call(
        functools.partial(_mm_kernel, has_norm=has_norm, has_bias=has_bias, has_res=has_res),
        out_shape=jax.ShapeDtypeStruct((m, n), out_dtype),
        grid=(m // tm, n // tn),
        in_specs=specs,
        out_specs=pl.BlockSpec((tm, tn), lambda i, j: (i, j)),
        scratch_shapes=[pltpu.VMEM((tm, k), BF16)] if has_norm else [],
        compiler_params=_params(("parallel", "arbitrary")),
        name=name,
    )(*args)


def _ffn_kernel(*refs, has_final):
    if has_final:
        x_ref, g_ref, wu_ref, wd_ref, gf_ref, o_ref, xn_ref = refs
    else:
        x_ref, g_ref, wu_ref, wd_ref, o_ref, xn_ref = refs
    f = pl.program_id(1)

    @pl.when(f == 0)
    def _():
        x = x_ref[...]
        xn_ref[...] = _rms(x, g_ref[...]).astype(BF16)
        o_ref[...] = x

    h = jnp.maximum(_dot(xn_ref[...], wu_ref[...]), 0.0)
    o_ref[...] += _dot((h * h).astype(BF16), wd_ref[...])

    if has_final:
        @pl.when(f == pl.num_programs(1) - 1)
        def _():
            o_ref[...] = _rms(o_ref[...], gf_ref[...])


def _ffn(x, g, w_up, w_down, layer, final_g=None, *, tm=512, tf=1024):
    m, d = x.shape
    ff = w_up.shape[-1]
    tm, tf = _tile(m, tm), _tile(ff, tf)
    has_final = final_g is not None
    args = [x, g.reshape(1, d), w_up, w_down]
    specs = [
        pl.BlockSpec((tm, d), lambda i, f: (i, 0)),
        pl.BlockSpec((1, d), lambda i, f: (0, 0)),
        pl.BlockSpec((None, d, tf), lambda i, f: (layer, 0, f)),
        pl.BlockSpec((None, tf, d), lambda i, f: (layer, f, 0)),
    ]
    if has_final:
        args.append(final_g.reshape(1, d))
        specs.append(pl.BlockSpec((1, d), lambda i, f: (0, 0)))
    return pl.pallas_call(
        functools.partial(_ffn_kernel, has_final=has_final),
        out_shape=jax.ShapeDtypeStruct((m, d), F32),
        grid=(m // tm, ff // tf),
        in_specs=specs,
        out_specs=pl.BlockSpec((tm, d), lambda i, f: (i, 0)),
        scratch_shapes=[pltpu.VMEM((tm, d), BF16)],
        compiler_params=_params(("parallel", "arbitrary")),
        name="ffn",
    )(*args)


def _mix_kernel(*refs, tiles_per_seq, has_vres, ranks):
    it = iter(refs)
    x_ref, p_ref, g_ref, mu_ref, a_ref = (next(it) for _ in range(5))
    w2_ref, a2_ref, g2_ref = next(it), next(it), next(it)
    v2_ref = next(it) if has_vres else None
    w0_ref, a0_ref = next(it), next(it)
    v0_ref = next(it) if has_vres else None
    xr_ref, xk_ref, xv_ref, w_ref, ao_ref, go_ref = (next(it) for _ in range(6))
    vg_ref = next(it) if has_vres else None

    g = g_ref[...]
    h = _rms(x_ref[...], g)
    hp = _rms(p_ref[...], g)[SUBLANES - 1:SUBLANES, :]
    hp = jnp.where(pl.program_id(0) % tiles_per_seq == 0, 0.0, hp)
    row = lax.broadcasted_iota(jnp.int32, h.shape, 0)
    shifted = jnp.where(row == 0, hp, pltpu.roll(h, 1, 0))
    xx = shifted - h

    def mix(c):
        return (h + xx * mu_ref[c:c + 1, :]).astype(BF16)

    rw, ra, rg, rv = ranks
    xr_ref[...] = mix(0)
    xk_ref[...] = mix(2)
    xv = mix(3)
    xv_ref[...] = xv
    mid = jnp.tanh(_dot(mix(1), a_ref[:, :rw])).astype(BF16)
    w_ref[...] = w0_ref[...] + _dot(mid, w2_ref[...])
    mid = _dot(mix(4), a_ref[:, rw:rw + ra]).astype(BF16)
    ao_ref[...] = jax.nn.sigmoid(a0_ref[...] + _dot(mid, a2_ref[...]))
    mid = jax.nn.sigmoid(_dot(mix(5), a_ref[:, rw + ra:rw + ra + rg])).astype(BF16)
    go_ref[...] = _dot(mid, g2_ref[...]).astype(BF16)
    if has_vres:
        mid = _dot(xv, a_ref[:, rw + ra + rg:]).astype(BF16)
        vg_ref[...] = jax.nn.sigmoid(v0_ref[...] + _dot(mid, v2_ref[...]))


def _pad_rank(a, b):
    pad = -a.shape[1] % LANES
    return jnp.pad(a, ((0, 0), (0, pad))).astype(BF16), jnp.pad(b, ((0, pad), (0, 0))).astype(BF16)


def _rwkv_mix(x, g, mu, seq_len, w0, w1, w2, a0, a1, a2, g1, g2, vres, *, tm=256):
    m, d = x.shape
    tm = _tile(seq_len, tm)
    blk = tm // SUBLANES
    has_vres = vres is not None
    pairs = [_pad_rank(w1, w2), _pad_rank(a1, a2), _pad_rank(g1, g2)]
    biases = [w0, a0]
    if has_vres:
        pairs.append(_pad_rank(vres[1], vres[2]))
        biases.append(vres[0])
    first = jnp.concatenate([p[0] for p in pairs], axis=1)
    ranks = tuple(p[0].shape[1] for p in pairs) + ((0,) if not has_vres else ())
    row_spec = pl.BlockSpec((tm, d), lambda i: (i, 0))
    full = lambda arr: pl.BlockSpec(arr.shape, lambda i: (0, 0))
    seconds = [p[1] for p in pairs]
    biases = [b.reshape(1, d) for b in biases]
    out_dtypes = [BF16, BF16, BF16, F32, F32, BF16] + ([F32] if has_vres else [])
    outs = pl.pallas_call(
        functools.partial(_mix_kernel, tiles_per_seq=seq_len // tm, has_vres=has_vres, ranks=ranks),
        out_shape=[jax.ShapeDtypeStruct((m, d), dt) for dt in out_dtypes],
        grid=(m // tm,),
        in_specs=[
            row_spec,
            pl.BlockSpec((SUBLANES, d), lambda i: (jnp.maximum(i * blk - 1, 0), 0)),
            pl.BlockSpec((1, d), lambda i: (0, 0)),
            full(mu), full(first),
        ] + [full(s) for s in seconds] + [full(b) for b in biases],
        out_specs=[row_spec] * len(out_dtypes),
        compiler_params=_params(("parallel",)),
        name="rwkv_mix",
    )(x, x, g.reshape(1, d), mu, first, *seconds, *biases)
    return outs


def _scan_kernel(*refs, has_vres, n_chunks, n_pairs):
    if has_vres:
        (r_ref, k_ref, v_ref, w_ref, a_ref, g_ref, vf_ref, vg_ref,
         kk_ref, ka_ref, rk_ref, lw_ref, lb_ref, o_ref, ht_ref) = refs
    else:
        (r_ref, k_ref, v_ref, w_ref, a_ref, g_ref,
         kk_ref, ka_ref, rk_ref, lw_ref, lb_ref, o_ref, ht_ref) = refs
    c2 = 2 * CHUNK

    @pl.when(pl.program_id(2) == 0)
    def _():
        ht_ref[...] = jnp.zeros_like(ht_ref)

    lane = lax.broadcasted_iota(jnp.int32, (CHUNK, LANES), 1)
    head0 = lane < HEAD

    def stack(x):
        return jnp.concatenate([jnp.where(head0, x, 0.0), jnp.where(head0, 0.0, x)], axis=0).astype(BF16)

    ri = lax.broadcasted_iota(jnp.int32, (LANES, LANES), 0)
    ci = lax.broadcasted_iota(jnp.int32, (LANES, LANES), 1)

    def head_sum(x):
        s0 = jnp.sum(jnp.where(head0, x, 0.0), axis=-1, keepdims=True)
        s1 = jnp.sum(jnp.where(head0, 0.0, x), axis=-1, keepdims=True)
        return jnp.where(head0, s0, s1)

    eye = (ri == ci).astype(F32)
    ti = lax.broadcasted_iota(jnp.int32, (CHUNK, CHUNK), 0)
    si = lax.broadcasted_iota(jnp.int32, (CHUNK, CHUNK), 1)
    tri = (ti >= si).astype(BF16)
    rt = lax.broadcasted_iota(jnp.int32, (c2, 2 * c2), 0) % CHUNK
    cs = lax.broadcasted_iota(jnp.int32, (c2, 2 * c2), 1) % CHUNK
    strict = rt > cs
    incl = rt >= cs

    def pair_chunk(rows, p):
        cols = slice(p * LANES, (p + 1) * LANES)
        r, k, v, a = r_ref[rows, cols], k_ref[rows, cols], v_ref[rows, cols], a_ref[rows, cols]
        k_k, k_a, r_k = kk_ref[:, cols], ka_ref[:, cols], rk_ref[:, cols]
        if has_vres:
            v = v + (vf_ref[rows, cols] - v) * vg_ref[rows, cols]
        lw = -DECAY_SCALE * jax.nn.sigmoid(w_ref[rows, cols])
        lp = _dot(tri, _split_hl(lw))
        lp = lp[:, :LANES] + lp[:, LANES:]
        kk = k * k_k
        norm = jnp.sqrt(head_sum(kk * kk))
        yield
        lp_end = lp[CHUNK - 1:CHUNK, :]
        kk = kk / jnp.maximum(norm, 1e-12)
        kmod = k * (1.0 + (a - 1.0) * k_a)
        bb = kk * a

        en = jnp.exp(-lp)
        to_end = jnp.exp(lp_end - lp)
        r_t = r * jnp.exp(lp)
        a_t = -kk * jnp.exp(lp - lw)
        lhs = jnp.concatenate([stack(a_t), stack(r_t)], axis=0)
        rhs = jnp.concatenate([stack(bb * en), stack(kmod * en)], axis=0)
        bk_end = jnp.concatenate([stack(bb * to_end), stack(kmod * to_end)], axis=0)
        vs = stack(v)

        gram = lax.dot_general(lhs, rhs, _NT, preferred_element_type=F32)
        yield
        top =jnp.where(strict, gram[:c2, :], 0.0)
        bot = jnp.where(incl, gram[c2:, :], 0.0).astype(BF16)
        n_ab = top[:, :c2]
        n_ak = top[:, c2:].astype(BF16)

        inv = eye + n_ab
        pw = n_ab.astype(BF16)
        pw = _dot(pw, pw).astype(BF16)
        yield
        for _ in range(4):
            prod = _dot(pw, jnp.concatenate([pw, inv.astype(BF16)], axis=1))
            yield
            pw = prod[:, :c2].astype(BF16)
            inv = inv + prod[:, c2:]
        inv = inv + _dot(pw, inv.astype(BF16))

        ht = ht_ref[p]
        state_proj = lax.dot_general(lhs, ht.astype(BF16), _NT, preferred_element_type=F32)
        x = state_proj[:c2, :] + _dot(n_ak, vs)
        yield
        us = _dot(inv.astype(BF16), x.astype(BF16)).astype(BF16)
        yield
        uv = jnp.concatenate([us, vs], axis=0)
        ys = state_proj[c2:, :] + _dot(bot, uv)
        ht_ref[p] = ht * jnp.exp(lp_end) + lax.dot_general(uv, bk_end, _TN, preferred_element_type=F32)
        yield
        y = ys[:CHUNK, :] + ys[CHUNK:, :]

        mean = head_sum(y) * (1.0 / HEAD)
        yield
        yc = y - mean
        var = head_sum(yc * yc) * (1.0 / HEAD)
        bonus = head_sum(r * kmod * r_k)
        yield
        out = yc * lax.rsqrt(var + GN_EPS) * lw_ref[:, cols] + lb_ref[:, cols]
        out = out + bonus * v
        o_ref[rows, cols] = (out * g_ref[rows, cols]).astype(o_ref.dtype)
        yield

    def chunk(c, carry):
        rows = pl.ds(pl.multiple_of(c * CHUNK, CHUNK), CHUNK)
        for _ in zip(*[pair_chunk(rows, p) for p in range(n_pairs)]):
            pass
        return carry

    lax.fori_loop(0, n_chunks, chunk, 0)


def _rwkv_scan(r, k, v, w, a, g, k_k, k_a, r_k, ln_w, ln_b, seq_len, vres=None, *, tc=256, n_pairs=16):
    m, d = r.shape
    tc = _tile(seq_len, tc)
    nt = seq_len // tc
    n_pairs = _tile(d // LANES, n_pairs)
    width = n_pairs * LANES
    row_spec = pl.BlockSpec((tc, width), lambda b, h, c: (b * nt + c, h))
    par_spec = pl.BlockSpec((1, width), lambda b, h, c: (0, h))
    seqs = [r, k, v, w, a, g] + (list(vres) if vres is not None else [])
    pars = [p.reshape(1, d) for p in (k_k, k_a, r_k, ln_w, ln_b)]
    return pl.pallas_call(
        functools.partial(_scan_kernel, has_vres=vres is not None, n_chunks=tc // CHUNK, n_pairs=n_pairs),
        out_shape=jax.ShapeDtypeStruct((m, d), BF16),
        grid=(m // seq_len, d // width, nt),
        in_specs=[row_spec] * len(seqs) + [par_spec] * len(pars),
        out_specs=row_spec,
        scratch_shapes=[pltpu.VMEM((n_pairs, LANES, LANES), F32)],
        compiler_params=_params(("parallel", "parallel", "arbitrary")),
        name="rwkv_scan",
    )(*seqs, *pars)


def _swa_kernel(sink_ref, q_ref, kp_ref, kc_ref, vp_ref, vc_ref, o_ref, *, n_kv, q_per_kv, blocks_per_seq):
    first = pl.program_id(0) % blocks_per_seq == 0
    lane = lax.broadcasted_iota(jnp.int32, (ATT_BLOCK, LANES), 1)
    even = lane < HEAD
    qi = lax.broadcasted_iota(jnp.int32, (ATT_BLOCK, 2 * ATT_BLOCK), 0)
    kc = lax.broadcasted_iota(jnp.int32, (ATT_BLOCK, 2 * ATT_BLOCK), 1)
    valid = (kc > qi) & (kc <= qi + ATT_BLOCK) & ((kc >= ATT_BLOCK) | jnp.logical_not(first))
    for gi in range(n_kv):
        cols = slice(gi * LANES, (gi + 1) * LANES)
        keys = jnp.concatenate([kp_ref[:, cols], kc_ref[:, cols]], axis=0)
        vals = jnp.concatenate([vp_ref[:, cols], vc_ref[:, cols]], axis=0)
        for pi in range(q_per_kv // 2):
            tile = gi * (q_per_kv // 2) + pi
            q2 = q_ref[:, tile * LANES:(tile + 1) * LANES]
            outs = []
            for half in range(2):
                sink = sink_ref[2 * tile + half]
                qh = jnp.where(even if half == 0 else jnp.logical_not(even), q2, jnp.zeros_like(q2))
                logits = lax.dot_general(qh, keys, _NT, preferred_element_type=F32) * (HEAD ** -0.5)
                logits = jnp.where(valid, logits, -jnp.inf)
                mx = jnp.maximum(jnp.max(logits, axis=-1, keepdims=True), sink)
                p = jnp.exp(logits - mx)
                denom = jnp.sum(p, axis=-1, keepdims=True) + jnp.exp(sink - mx)
                outs.append(_dot((p / denom).astype(BF16), vals))
            o_ref[:, tile * LANES:(tile + 1) * LANES] = jnp.where(even, outs[0], outs[1]).astype(o_ref.dtype)


def _swa(qkv, sinks, seq_len, d, n_kv):
    m = qkv.shape[0]
    n_q = d // HEAD
    q_tiles = d // LANES
    blocks_per_seq = seq_len // ATT_BLOCK
    kw = n_kv * LANES
    k_blk = d // kw
    prev = lambda i: jnp.maximum(i - 1, 0)
    return pl.pallas_call(
        functools.partial(_swa_kernel, n_kv=n_kv, q_per_kv=n_q // n_kv, blocks_per_seq=blocks_per_seq),
        out_shape=jax.ShapeDtypeStruct((m, d), BF16),
        grid=(m // ATT_BLOCK,),
        in_specs=[
            pl.BlockSpec(memory_space=pltpu.SMEM),
            pl.BlockSpec((ATT_BLOCK, d), lambda i: (i, 0)),
            pl.BlockSpec((ATT_BLOCK, kw), lambda i: (prev(i), k_blk)),
            pl.BlockSpec((ATT_BLOCK, kw), lambda i: (i, k_blk)),
            pl.BlockSpec((ATT_BLOCK, kw), lambda i: (prev(i), k_blk + 1)),
            pl.BlockSpec((ATT_BLOCK, kw), lambda i: (i, k_blk + 1)),
        ],
        out_specs=pl.BlockSpec((ATT_BLOCK, d), lambda i: (i, 0)),
        compiler_params=_params(("parallel",)),
        name="swa",
    )(sinks, qkv, qkv, qkv, qkv, qkv)


def _conv_kernel(x_ref, p_ref, g_ref, wb_ref, wc_ref, wh_ref, cw_ref, o_ref, xn_ref, *, tiles_per_seq):
    @pl.when(pl.program_id(1) == 0)
    def _():
        g = g_ref[...]
        xn_ref[:SUBLANES, :] = _rms(p_ref[...], g).astype(BF16)
        xn_ref[SUBLANES:, :] = _rms(x_ref[...], g).astype(BF16)

    xn = xn_ref[...]
    u = _dot(xn, wc_ref[...]) * _dot(xn, wh_ref[...])
    row = lax.broadcasted_iota(jnp.int32, u.shape, 0)
    first = pl.program_id(0) % tiles_per_seq == 0
    u = jnp.where((row < SUBLANES) & first, 0.0, u)
    cw = cw_ref[...]
    uc = (cw[2:3, :] * u + cw[1:2, :] * pltpu.roll(u, 1, 0) + cw[0:1, :] * pltpu.roll(u, 2, 0))[SUBLANES:, :]
    bg = _dot(xn[SUBLANES:, :], wb_ref[...])
    o_ref[...] = (bg * uc).astype(o_ref.dtype)


def _conv_front(x, g, w_in, conv_w, layer, seq_len, *, tm=512, tn=512):
    m, d = x.shape
    tm, tn = _tile(seq_len, tm), _tile(d, tn)
    nj = d // tn
    blk = tm // SUBLANES
    return pl.pallas_call(
        functools.partial(_conv_kernel, tiles_per_seq=seq_len // tm),
        out_shape=jax.ShapeDtypeStruct((m, d), BF16),
        grid=(m // tm, nj),
        in_specs=[
            pl.BlockSpec((tm, d), lambda i, j: (i, 0)),
            pl.BlockSpec((SUBLANES, d), lambda i, j: (jnp.maximum(i * blk - 1, 0), 0)),
            pl.BlockSpec((1, d), lambda i, j: (0, 0)),
            pl.BlockSpec((None, d, tn), lambda i, j: (layer, 0, j)),
            pl.BlockSpec((None, d, tn), lambda i, j: (layer, 0, nj + j)),
            pl.BlockSpec((None, d, tn), lambda i, j: (layer, 0, 2 * nj + j)),
            pl.BlockSpec((None, conv_w.shape[1], tn), lambda i, j: (layer, 0, j)),
        ],
        out_specs=pl.BlockSpec((tm, tn), lambda i, j: (i, j)),
        scratch_shapes=[pltpu.VMEM((SUBLANES + tm, d), BF16)],
        compiler_params=_params(("parallel", "arbitrary")),
        name="conv_front",
    )(x, x, g.reshape(1, d), w_in, w_in, w_in, conv_w)


def _rwkv_layer(x, seq_len, g_norm, mu, w_rkv, w_o, layer, w0, w1, w2, a0, a1, a2, g1, g2,
                k_k, k_a, r_k, ln_w, ln_b, v_first, vres):
    outs = _rwkv_mix(x, g_norm, mu, seq_len, w0, w1, w2, a0, a1, a2, g1, g2, vres)
    xr, xk, xv, w, a, g = outs[:6]
    r = _mm(xr, w_rkv, w_lead=(layer, 0), name="rwkv_r")
    k = _mm(xk, w_rkv, w_lead=(layer, 1), name="rwkv_k")
    v = _mm(xv, w_rkv, w_lead=(layer, 2), name="rwkv_v")
    if vres is None:
        scan_vres = None
        v_first = v
    else:
        scan_vres = (v_first, outs[6])
    yg = _rwkv_scan(r, k, v, w, a, g, k_k, k_a, r_k.reshape(-1), ln_w, ln_b, seq_len, scan_vres)
    return _mm(yg, w_o, w_lead=(layer,), res=x, name="rwkv_o"), v_first


def _swa_layer(x, seq_len, g_norm, w_qkv, b_qkv, w_o, layer, b_o, sinks):
    d = x.shape[1]
    n_kv = (w_qkv.shape[1] - d) // (2 * HEAD)
    kvd = n_kv * HEAD

    def dup(t):
        lead = t.shape[:-1]
        t = t.reshape(lead + (n_kv, 1, HEAD))
        return jnp.broadcast_to(t, lead + (n_kv, 2, HEAD)).reshape(lead + (2 * kvd,))

    w_cat = jnp.concatenate([w_qkv[:, :d], dup(w_qkv[:, d:d + kvd]), dup(w_qkv[:, d + kvd:])], axis=1)
    b_cat = jnp.concatenate([b_qkv[:d], dup(b_qkv[d:d + kvd]), dup(b_qkv[d + kvd:])])
    qkv = _mm(x, w_cat.astype(BF16), norm_g=g_norm, bias=b_cat, out_dtype=BF16, name="swa_qkv")
    o = _swa(qkv, sinks, seq_len, d, n_kv)
    return _mm(o, w_o, w_lead=(layer,), bias=b_o, res=x, name="swa_o")


def _conv_layer(x, seq_len, g_norm, w_in, conv_w, w_out, layer):
    gated = _conv_front(x, g_norm, w_in, conv_w, layer, seq_len)
    return _mm(gated, w_out, w_lead=(layer,), res=x, name="conv_o")


def kernel(x, norm_mix, norm_ffn, norm_final, rwkv_mu, rwkv_w_rkv, rwkv_w_o, rwkv_w0, rwkv_w1, rwkv_w2, rwkv_a0, rwkv_a1, rwkv_a2, rwkv_v0, rwkv_v1, rwkv_v2, rwkv_g1, rwkv_g2, rwkv_k_k, rwkv_k_a, rwkv_r_k, rwkv_lnx_w, rwkv_lnx_b, swa_w_qkv, swa_b_qkv, swa_w_o, swa_b_o, swa_sinks, conv_w_in, conv_w, conv_w_out, mlp_w_up, mlp_w_down):
    bsz, seq_len, d = x.shape
    depth = norm_mix.shape[0]
    xf = x.reshape(bsz * seq_len, d)
    rwkv_w_rkv, rwkv_w_o, swa_w_o, conv_w_in, conv_w_out, mlp_w_up, mlp_w_down = (
        t.astype(BF16) for t in (rwkv_w_rkv, rwkv_w_o, swa_w_o, conv_w_in, conv_w_out, mlp_w_up, mlp_w_down))
    v_first = None
    ia = ib = ic = 0
    for i in range(depth):
        kind = i % 3
        if kind == 0:
            vres = None if ia == 0 else (rwkv_v0[ia - 1], rwkv_v1[ia - 1], rwkv_v2[ia - 1])
            xf, v_first = _rwkv_layer(
                xf, seq_len, norm_mix[i], rwkv_mu[ia], rwkv_w_rkv, rwkv_w_o, ia,
                rwkv_w0[ia], rwkv_w1[ia], rwkv_w2[ia], rwkv_a0[ia], rwkv_a1[ia], rwkv_a2[ia],
                rwkv_g1[ia], rwkv_g2[ia], rwkv_k_k[ia], rwkv_k_a[ia], rwkv_r_k[ia],
                rwkv_lnx_w[ia], rwkv_lnx_b[ia], v_first, vres)
            ia += 1
        elif kind == 1:
            xf = _swa_layer(xf, seq_len, norm_mix[i], swa_w_qkv[ib], swa_b_qkv[ib],
                            swa_w_o, ib, swa_b_o[ib], swa_sinks[ib])
            ib += 1
        else:
            xf = _conv_layer(xf, seq_len, norm_mix[i], conv_w_in, conv_w, conv_w_out, ic)
            ic += 1
        xf = _ffn(xf, norm_ffn[i], mlp_w_up, mlp_w_down, i, norm_final if i == depth - 1 else None)
    return xf.reshape(bsz, seq_len, d)
```

```python
import functools

import jax
import jax.numpy as jnp
from jax import lax
from jax.experimental import pallas as pl
from jax.experimental.pallas import tpu as pltpu

F32 = jnp.float32
BF16 = jnp.bfloat16

HEAD = 64
LANES = 128
SUBLANES = 8
CHUNK = 64
ATT_BLOCK = 128
RMS_EPS = 1e-6
GN_EPS = 64e-5
DECAY_SCALE = 0.6065306597126334
VMEM_LIMIT = 56 * 1024 * 1024

_NT = (((1,), (1,)), ((), ()))
_TN = (((0,), (0,)), ((), ()))


def _tile(n, pref):
    t = min(n, pref)
    while n % t:
        t //= 2
    return t


def _params(sem):
    return pltpu.CompilerParams(dimension_semantics=sem, vmem_limit_bytes=VMEM_LIMIT)


def _rms(x, g):
    ms = jnp.mean(x * x, axis=-1, keepdims=True)
    return x * lax.rsqrt(ms + RMS_EPS) * g


def _dot(a, b):
    return jnp.dot(a, b, preferred_element_type=F32)


def _split_hl(x):
    hi = x.astype(BF16)
    lo = (x - hi.astype(F32)).astype(BF16)
    return jnp.concatenate([hi, lo], axis=1)


def _mm_kernel(*refs, has_norm, has_bias, has_res):
    it = iter(refs)
    x_ref = next(it)
    g_ref = next(it) if has_norm else None
    w_ref = next(it)
    b_ref = next(it) if has_bias else None
    r_ref = next(it) if has_res else None
    o_ref = next(it)
    xn_ref = next(it) if has_norm else None

    if has_norm:
        @pl.when(pl.program_id(1) == 0)
        def _():
            xn_ref[...] = _rms(x_ref[...], g_ref[...]).astype(BF16)
        xb = xn_ref[...]
    else:
        xb = x_ref[...]
    acc = _dot(xb, w_ref[...])
    if has_bias:
        acc = acc + b_ref[...]
    if has_res:
        acc = acc + r_ref[...]
    o_ref[...] = acc.astype(o_ref.dtype)


def _mm(x, w, *, w_lead=(), norm_g=None, bias=None, res=None, out_dtype=F32, tm=512, tn=4096, name="mm"):
    m, k = x.shape
    n = w.shape[-1]
    squeezed = (None,) * len(w_lead)
    tm, tn = _tile(m, tm), _tile(n, tn)
    has_norm, has_bias, has_res = norm_g is not None, bias is not None, res is not None
    args, specs = [x], [pl.BlockSpec((tm, k), lambda i, j: (i, 0))]
    if has_norm:
        args.append(norm_g.reshape(1, k))
        specs.append(pl.BlockSpec((1, k), lambda i, j: (0, 0)))
    args.append(w)
    specs.append(pl.BlockSpec(squeezed + (k, tn), lambda i, j: w_lead + (0, j)))
    if has_bias:
        args.append(bias.reshape(1, n))
        specs.append(pl.BlockSpec((1, tn), lambda i, j: (0, j)))
    if has_res:
        args.append(res)
        specs.append(pl.BlockSpec((tm, tn), lambda i, j: (i, j)))
    return pl.pallas_call(
        functools.partial(_mm_kernel, has_norm=has_norm, has_bias=has_bias, has_res=has_res),
        out_shape=jax.ShapeDtypeStruct((m, n), out_dtype),
        grid=(m // tm, n // tn),
        in_specs=specs,
        out_specs=pl.BlockSpec((tm, tn), lambda i, j: (i, j)),
        scratch_shapes=[pltpu.VMEM((tm, k), BF16)] if has_norm else [],
        compiler_params=_params(("parallel", "arbitrary")),
        name=name,
    )(*args)


def _ffn_kernel(*refs, has_final):
    if has_final:
        x_ref, g_ref, wu_ref, wd_ref, gf_ref, o_ref, xn_ref = refs
    else:
        x_ref, g_ref, wu_ref, wd_ref, o_ref, xn_ref = refs
    f = pl.program_id(1)

    @pl.when(f == 0)
    def _():
        x = x_ref[...]
        xn_ref[...] = _rms(x, g_ref[...]).astype(BF16)
        o_ref[...] = x

    h = jnp.maximum(_dot(xn_ref[...], wu_ref[...]), 0.0)
    o_ref[...] += _dot((h * h).astype(BF16), wd_ref[...])

    if has_final:
        @pl.when(f == pl.num_programs(1) - 1)
        def _():
            o_ref[...] = _rms(o_ref[...], gf_ref[...])


def _ffn(x, g, w_up, w_down, layer, final_g=None, *, tm=512, tf=1024):
    m, d = x.shape
    ff = w_up.shape[-1]
    tm, tf = _tile(m, tm), _tile(ff, tf)
    has_final = final_g is not None
    args = [x, g.reshape(1, d), w_up, w_down]
    specs = [
        pl.BlockSpec((tm, d), lambda i, f: (i, 0)),
        pl.BlockSpec((1, d), lambda i, f: (0, 0)),
        pl.BlockSpec((None, d, tf), lambda i, f: (layer, 0, f)),
        pl.BlockSpec((None, tf, d), lambda i, f: (layer, f, 0)),
    ]
    if has_final:
        args.append(final_g.reshape(1, d))
        specs.append(pl.BlockSpec((1, d), lambda i, f: (0, 0)))
    return pl.pallas_call(
        functools.partial(_ffn_kernel, has_final=has_final),
        out_shape=jax.ShapeDtypeStruct((m, d), F32),
        grid=(m // tm, ff // tf),
        in_specs=specs,
        out_specs=pl.BlockSpec((tm, d), lambda i, f: (i, 0)),
        scratch_shapes=[pltpu.VMEM((tm, d), BF16)],
        compiler_params=_params(("parallel", "arbitrary")),
        name="ffn",
    )(*args)


def _mix_kernel(*refs, tiles_per_seq, has_vres, ranks):
    it = iter(refs)
    x_ref, p_ref, g_ref, mu_ref, a_ref = (next(it) for _ in range(5))
    w2_ref, a2_ref, g2_ref = next(it), next(it), next(it)
    v2_ref = next(it) if has_vres else None
    w0_ref, a0_ref = next(it), next(it)
    v0_ref = next(it) if has_vres else None
    xr_ref, xk_ref, xv_ref, w_ref, ao_ref, go_ref = (next(it) for _ in range(6))
    vg_ref = next(it) if has_vres else None

    g = g_ref[...]
    h = _rms(x_ref[...], g)
    hp = _rms(p_ref[...], g)[SUBLANES - 1:SUBLANES, :]
    hp = jnp.where(pl.program_id(0) % tiles_per_seq == 0, 0.0, hp)
    row = lax.broadcasted_iota(jnp.int32, h.shape, 0)
    shifted = jnp.where(row == 0, hp, pltpu.roll(h, 1, 0))
    xx = shifted - h

    def mix(c):
        return (h + xx * mu_ref[c:c + 1, :]).astype(BF16)

    rw, ra, rg, rv = ranks
    xr_ref[...] = mix(0)
    xk_ref[...] = mix(2)
    xv = mix(3)
    xv_ref[...] = xv
    mid = jnp.tanh(_dot(mix(1), a_ref[:, :rw])).astype(BF16)
    w_ref[...] = w0_ref[...] + _dot(mid, w2_ref[...])
    mid = _dot(mix(4), a_ref[:, rw:rw + ra]).astype(BF16)
    ao_ref[...] = jax.nn.sigmoid(a0_ref[...] + _dot(mid, a2_ref[...]))
    mid = jax.nn.sigmoid(_dot(mix(5), a_ref[:, rw + ra:rw + ra + rg])).astype(BF16)
    go_ref[...] = _dot(mid, g2_ref[...]).astype(BF16)
    if has_vres:
        mid = _dot(xv, a_ref[:, rw + ra + rg:]).astype(BF16)
        vg_ref[...] = jax.nn.sigmoid(v0_ref[...] + _dot(mid, v2_ref[...]))


def _pad_rank(a, b):
    pad = -a.shape[1] % LANES
    return jnp.pad(a, ((0, 0), (0, pad))).astype(BF16), jnp.pad(b, ((0, pad), (0, 0))).astype(BF16)


def _rwkv_mix(x, g, mu, seq_len, w0, w1, w2, a0, a1, a2, g1, g2, vres, *, tm=256):
    m, d = x.shape
    tm = _tile(seq_len, tm)
    blk = tm // SUBLANES
    has_vres = vres is not None
    pairs = [_pad_rank(w1, w2), _pad_rank(a1, a2), _pad_rank(g1, g2)]
    biases = [w0, a0]
    if has_vres:
        pairs.append(_pad_rank(vres[1], vres[2]))
        biases.append(vres[0])
    first = jnp.concatenate([p[0] for p in pairs], axis=1)
    ranks = tuple(p[0].shape[1] for p in pairs) + ((0,) if not has_vres else ())
    row_spec = pl.BlockSpec((tm, d), lambda i: (i, 0))
    full = lambda arr: pl.BlockSpec(arr.shape, lambda i: (0, 0))
    seconds = [p[1] for p in pairs]
    biases = [b.reshape(1, d) for b in biases]
    out_dtypes = [BF16, BF16, BF16, F32, F32, BF16] + ([F32] if has_vres else [])
    outs = pl.pallas_call(
        functools.partial(_mix_kernel, tiles_per_seq=seq_len // tm, has_vres=has_vres, ranks=ranks),
        out_shape=[jax.ShapeDtypeStruct((m, d), dt) for dt in out_dtypes],
        grid=(m // tm,),
        in_specs=[
            row_spec,
            pl.BlockSpec((SUBLANES, d), lambda i: (jnp.maximum(i * blk - 1, 0), 0)),
            pl.BlockSpec((1, d), lambda i: (0, 0)),
            full(mu), full(first),
        ] + [full(s) for s in seconds] + [full(b) for b in biases],
        out_specs=[row_spec] * len(out_dtypes),
        compiler_params=_params(("parallel",)),
        name="rwkv_mix",
    )(x, x, g.reshape(1, d), mu, first, *seconds, *biases)
    return outs


def _scan_kernel(*refs, has_vres, n_chunks, n_pairs):
    if has_vres:
        (r_ref, k_ref, v_ref, w_ref, a_ref, g_ref, vf_ref, vg_ref,
         kk_ref, ka_ref, rk_ref, lw_ref, lb_ref, o_ref, ht_ref) = refs
    else:
        (r_ref, k_ref, v_ref, w_ref, a_ref, g_ref,
         kk_ref, ka_ref, rk_ref, lw_ref, lb_ref, o_ref, ht_ref) = refs
    c2 = 2 * CHUNK

    @pl.when(pl.program_id(2) == 0)
    def _():
        ht_ref[...] = jnp.zeros_like(ht_ref)

    lane = lax.broadcasted_iota(jnp.int32, (CHUNK, LANES), 1)
    head0 = lane < HEAD

    def both(x, y):
        return jnp.concatenate([x, y], axis=0)

    def split(x0, x1):
        return jnp.where(head0, x0, x1)

    ri = lax.broadcasted_iota(jnp.int32, (LANES, LANES), 0)
    ci = lax.broadcasted_iota(jnp.int32, (LANES, LANES), 1)

    def head_sum(x):
        s0 = jnp.sum(jnp.where(head0, x, 0.0), axis=-1, keepdims=True)
        s1 = jnp.sum(jnp.where(head0, 0.0, x), axis=-1, keepdims=True)
        return jnp.where(head0, s0, s1)

    eye = (ri == ci).astype(F32)
    same_head = (ri // HEAD) == (ci // HEAD)
    ti = lax.broadcasted_iota(jnp.int32, (CHUNK, CHUNK), 0)
    si = lax.broadcasted_iota(jnp.int32, (CHUNK, CHUNK), 1)
    tri = (ti >= si).astype(BF16)
    ts = lax.broadcasted_iota(jnp.int32, (CHUNK, LANES), 0)
    strict = ts > lane % HEAD
    incl = ts >= lane % HEAD

    def pair_chunk(rows, p):
        cols = slice(p * LANES, (p + 1) * LANES)
        r, k, v, a = r_ref[rows, cols], k_ref[rows, cols], v_ref[rows, cols], a_ref[rows, cols]
        k_k, k_a, r_k = kk_ref[:, cols], ka_ref[:, cols], rk_ref[:, cols]
        if has_vres:
            v = v + (vf_ref[rows, cols] - v) * vg_ref[rows, cols]
        lw = -DECAY_SCALE * jax.nn.sigmoid(w_ref[rows, cols])
        lp = _dot(tri, _split_hl(lw))
        lp = lp[:, :LANES] + lp[:, LANES:]
        kk = k * k_k
        norm = jnp.sqrt(head_sum(kk * kk))
        yield
        lp_end = lp[CHUNK - 1:CHUNK, :]
        kk = kk / jnp.maximum(norm, 1e-12)
        kmod = k * (1.0 + (a - 1.0) * k_a)
        bb = kk * a

        en = jnp.exp(-lp)
        to_end = jnp.exp(lp_end - lp)
        r_t = r * jnp.exp(lp)
        a_t = -kk * jnp.exp(lp - lw)
        b_t = (bb * en).astype(BF16)
        k_t = (kmod * en).astype(BF16)
        v_b = v.astype(BF16)
        g0 = lax.dot_general(both(jnp.where(head0, a_t, 0.0), jnp.where(head0, r_t, 0.0)).astype(BF16),
                             both(b_t, k_t), _NT, preferred_element_type=F32)
        g1 = lax.dot_general(both(jnp.where(head0, 0.0, a_t), jnp.where(head0, 0.0, r_t)).astype(BF16),
                             both(k_t, b_t), _NT, preferred_element_type=F32)
        yield
        top0 = jnp.where(strict, g0[:CHUNK, :], 0.0)
        top1 = jnp.where(strict, g1[:CHUNK, :], 0.0)
        bot0 = jnp.where(incl, g0[CHUNK:, :], 0.0).astype(BF16)
        bot1 = jnp.where(incl, g1[CHUNK:, :], 0.0).astype(BF16)
        n_ab = both(jnp.where(head0, top0, 0.0), jnp.where(head0, 0.0, top1))
        n_ak = both(jnp.where(head0, 0.0, top0), jnp.where(head0, top1, 0.0)).astype(BF16)

        inv = eye + n_ab
        pw = n_ab.astype(BF16)
        pw = _dot(pw, pw).astype(BF16)
        yield
        for _ in range(CHUNK.bit_length() - 3):
            prod = _dot(pw, jnp.concatenate([pw, inv.astype(BF16)], axis=1))
            yield
            pw = prod[:, :c2].astype(BF16)
            inv = inv + prod[:, c2:]
        inv = inv + _dot(pw, inv.astype(BF16))

        ht = ht_ref[p]
        state_proj = lax.dot_general(both(a_t, r_t).astype(BF16), ht.astype(BF16), _NT,
                                     preferred_element_type=F32)
        xv = _dot(n_ak, both(v_b, v_b))
        yield
        sp_a = state_proj[:CHUNK, :]
        xs = both(jnp.where(head0, xv[:CHUNK, :] + sp_a, 0.0), jnp.where(head0, 0.0, xv[CHUNK:, :] + sp_a))
        us = _dot(inv.astype(BF16), xs.astype(BF16))
        yield
        u_b = (us[:CHUNK, :] + us[CHUNK:, :]).astype(BF16)
        y = state_proj[CHUNK:, :] + split(_dot(bot0, both(u_b, v_b)), _dot(bot1, both(v_b, u_b)))
        new = lax.dot_general(both(u_b, v_b), both((bb * to_end).astype(BF16), (kmod * to_end).astype(BF16)),
                              _TN, preferred_element_type=F32)
        ht_ref[p] = ht * jnp.exp(lp_end) + jnp.where(same_head, new, 0.0)
        yield

        mean = head_sum(y) * (1.0 / HEAD)
        yield
        yc = y - mean
        var = head_sum(yc * yc) * (1.0 / HEAD)
        bonus = head_sum(r * kmod * r_k)
        yield
        out = yc * lax.rsqrt(var + GN_EPS) * lw_ref[:, cols] + lb_ref[:, cols]
        out = out + bonus * v
        o_ref[rows, cols] = (out * g_ref[rows, cols]).astype(o_ref.dtype)
        yield

    def chunk(c, carry):
        rows = pl.ds(pl.multiple_of(c * CHUNK, CHUNK), CHUNK)
        for _ in zip(*[pair_chunk(rows, p) for p in range(n_pairs)]):
            pass
        return carry

    lax.fori_loop(0, n_chunks, chunk, 0)


def _rwkv_scan(r, k, v, w, a, g, k_k, k_a, r_k, ln_w, ln_b, seq_len, vres=None, *, tc=256, n_pairs=16):
    m, d = r.shape
    tc = _tile(seq_len, tc)
    nt = seq_len // tc
    n_pairs = _tile(d // LANES, n_pairs)
    width = n_pairs * LANES
    row_spec = pl.BlockSpec((tc, width), lambda b, h, c: (b * nt + c, h))
    par_spec = pl.BlockSpec((1, width), lambda b, h, c: (0, h))
    seqs = [r, k, v, w, a, g] + (list(vres) if vres is not None else [])
    pars = [p.reshape(1, d) for p in (k_k, k_a, r_k, ln_w, ln_b)]
    return pl.pallas_call(
        functools.partial(_scan_kernel, has_vres=vres is not None, n_chunks=tc // CHUNK, n_pairs=n_pairs),
        out_shape=jax.ShapeDtypeStruct((m, d), BF16),
        grid=(m // seq_len, d // width, nt),
        in_specs=[row_spec] * len(seqs) + [par_spec] * len(pars),
        out_specs=row_spec,
        scratch_shapes=[pltpu.VMEM((n_pairs, LANES, LANES), F32)],
        compiler_params=_params(("parallel", "parallel", "arbitrary")),
        name="rwkv_scan",
    )(*seqs, *pars)


def _swa_kernel(sink_ref, q_ref, kp_ref, kc_ref, vp_ref, vc_ref, o_ref, *, n_kv, q_per_kv, blocks_per_seq):
    first = pl.program_id(0) % blocks_per_seq == 0
    lane = lax.broadcasted_iota(jnp.int32, (ATT_BLOCK, LANES), 1)
    even = lane < HEAD
    qi = lax.broadcasted_iota(jnp.int32, (ATT_BLOCK, 2 * ATT_BLOCK), 0)
    kc = lax.broadcasted_iota(jnp.int32, (ATT_BLOCK, 2 * ATT_BLOCK), 1)
    valid = (kc > qi) & (kc <= qi + ATT_BLOCK) & ((kc >= ATT_BLOCK) | jnp.logical_not(first))
    for gi in range(n_kv):
        cols = slice(gi * LANES, (gi + 1) * LANES)
        keys = jnp.concatenate([kp_ref[:, cols], kc_ref[:, cols]], axis=0)
        vals = jnp.concatenate([vp_ref[:, cols], vc_ref[:, cols]], axis=0)
        for pi in range(q_per_kv // 2):
            tile = gi * (q_per_kv // 2) + pi
            q2 = q_ref[:, tile * LANES:(tile + 1) * LANES]
            outs = []
            for half in range(2):
                sink = sink_ref[2 * tile + half]
                qh = jnp.where(even if half == 0 else jnp.logical_not(even), q2, jnp.zeros_like(q2))
                logits = lax.dot_general(qh, keys, _NT, preferred_element_type=F32) * (HEAD ** -0.5)
                logits = jnp.where(valid, logits, -jnp.inf)
                mx = jnp.maximum(jnp.max(logits, axis=-1, keepdims=True), sink)
                p = jnp.exp(logits - mx)
                denom = jnp.sum(p, axis=-1, keepdims=True) + jnp.exp(sink - mx)
                outs.append(_dot((p / denom).astype(BF16), vals))
            o_ref[:, tile * LANES:(tile + 1) * LANES] = jnp.where(even, outs[0], outs[1]).astype(o_ref.dtype)


def _swa(qkv, sinks, seq_len, d, n_kv):
    m = qkv.shape[0]
    n_q = d // HEAD
    q_tiles = d // LANES
    blocks_per_seq = seq_len // ATT_BLOCK
    kw = n_kv * LANES
    k_blk = d // kw
    prev = lambda i: jnp.maximum(i - 1, 0)
    return pl.pallas_call(
        functools.partial(_swa_kernel, n_kv=n_kv, q_per_kv=n_q // n_kv, blocks_per_seq=blocks_per_seq),
        out_shape=jax.ShapeDtypeStruct((m, d), BF16),
        grid=(m // ATT_BLOCK,),
        in_specs=[
            pl.BlockSpec(memory_space=pltpu.SMEM),
            pl.BlockSpec((ATT_BLOCK, d), lambda i: (i, 0)),
            pl.BlockSpec((ATT_BLOCK, kw), lambda i: (prev(i), k_blk)),
            pl.BlockSpec((ATT_BLOCK, kw), lambda i: (i, k_blk)),
            pl.BlockSpec((ATT_BLOCK, kw), lambda i: (prev(i), k_blk + 1)),
            pl.BlockSpec((ATT_BLOCK, kw), lambda i: (i, k_blk + 1)),
        ],
        out_specs=pl.BlockSpec((ATT_BLOCK, d), lambda i: (i, 0)),
        compiler_params=_params(("parallel",)),
        name="swa",
    )(sinks, qkv, qkv, qkv, qkv, qkv)


def _conv_kernel(x_ref, p_ref, g_ref, wb_ref, wc_ref, wh_ref, cw_ref, o_ref, xn_ref, *, tiles_per_seq):
    @pl.when(pl.program_id(1) == 0)
    def _():
        g = g_ref[...]
        xn_ref[:SUBLANES, :] = _rms(p_ref[...], g).astype(BF16)
        xn_ref[SUBLANES:, :] = _rms(x_ref[...], g).astype(BF16)

    xn = xn_ref[...]
    u = _dot(xn, wc_ref[...]) * _dot(xn, wh_ref[...])
    row = lax.broadcasted_iota(jnp.int32, u.shape, 0)
    first = pl.program_id(0) % tiles_per_seq == 0
    u = jnp.where((row < SUBLANES) & first, 0.0, u)
    cw = cw_ref[...]
    uc = (cw[2:3, :] * u + cw[1:2, :] * pltpu.roll(u, 1, 0) + cw[0:1, :] * pltpu.roll(u, 2, 0))[SUBLANES:, :]
    bg = _dot(xn[SUBLANES:, :], wb_ref[...])
    o_ref[...] = (bg * uc).astype(o_ref.dtype)


def _conv_front(x, g, w_in, conv_w, layer, seq_len, *, tm=512, tn=512):
    m, d = x.shape
    tm, tn = _tile(seq_len, tm), _tile(d, tn)
    nj = d // tn
    blk = tm // SUBLANES
    return pl.pallas_call(
        functools.partial(_conv_kernel, tiles_per_seq=seq_len // tm),
        out_shape=jax.ShapeDtypeStruct((m, d), BF16),
        grid=(m // tm, nj),
        in_specs=[
            pl.BlockSpec((tm, d), lambda i, j: (i, 0)),
            pl.BlockSpec((SUBLANES, d), lambda i, j: (jnp.maximum(i * blk - 1, 0), 0)),
            pl.BlockSpec((1, d), lambda i, j: (0, 0)),
            pl.BlockSpec((None, d, tn), lambda i, j: (layer, 0, j)),
            pl.BlockSpec((None, d, tn), lambda i, j: (layer, 0, nj + j)),
            pl.BlockSpec((None, d, tn), lambda i, j: (layer, 0, 2 * nj + j)),
            pl.BlockSpec((None, conv_w.shape[1], tn), lambda i, j: (layer, 0, j)),
        ],
        out_specs=pl.BlockSpec((tm, tn), lambda i, j: (i, j)),
        scratch_shapes=[pltpu.VMEM((SUBLANES + tm, d), BF16)],
        compiler_params=_params(("parallel", "arbitrary")),
        name="conv_front",
    )(x, x, g.reshape(1, d), w_in, w_in, w_in, conv_w)


def _rwkv_layer(x, seq_len, g_norm, mu, w_rkv, w_o, layer, w0, w1, w2, a0, a1, a2, g1, g2,
                k_k, k_a, r_k, ln_w, ln_b, v_first, vres):
    outs = _rwkv_mix(x, g_norm, mu, seq_len, w0, w1, w2, a0, a1, a2, g1, g2, vres)
    xr, xk, xv, w, a, g = outs[:6]
    r = _mm(xr, w_rkv, w_lead=(layer, 0), name="rwkv_r")
    k = _mm(xk, w_rkv, w_lead=(layer, 1), name="rwkv_k")
    v = _mm(xv, w_rkv, w_lead=(layer, 2), name="rwkv_v")
    if vres is None:
        scan_vres = None
        v_first = v
    else:
        scan_vres = (v_first, outs[6])
    yg = _rwkv_scan(r, k, v, w, a, g, k_k, k_a, r_k.reshape(-1), ln_w, ln_b, seq_len, scan_vres)
    return _mm(yg, w_o, w_lead=(layer,), res=x, name="rwkv_o"), v_first


def _swa_layer(x, seq_len, g_norm, w_qkv, b_qkv, w_o, layer, b_o, sinks):
    d = x.shape[1]
    n_kv = (w_qkv.shape[1] - d) // (2 * HEAD)
    kvd = n_kv * HEAD

    def dup(t):
        lead = t.shape[:-1]
        t = t.reshape(lead + (n_kv, 1, HEAD))
        return jnp.broadcast_to(t, lead + (n_kv, 2, HEAD)).reshape(lead + (2 * kvd,))

    w_cat = jnp.concatenate([w_qkv[:, :d], dup(w_qkv[:, d:d + kvd]), dup(w_qkv[:, d + kvd:])], axis=1)
    b_cat = jnp.concatenate([b_qkv[:d], dup(b_qkv[d:d + kvd]), dup(b_qkv[d + kvd:])])
    qkv = _mm(x, w_cat.astype(BF16), norm_g=g_norm, bias=b_cat, out_dtype=BF16, name="swa_qkv")
    o = _swa(qkv, sinks, seq_len, d, n_kv)
    return _mm(o, w_o, w_lead=(layer,), bias=b_o, res=x, name="swa_o")


def _conv_layer(x, seq_len, g_norm, w_in, conv_w, w_out, layer):
    gated = _conv_front(x, g_norm, w_in, conv_w, layer, seq_len)
    return _mm(gated, w_out, w_lead=(layer,), res=x, name="conv_o")


def kernel(x, norm_mix, norm_ffn, norm_final, rwkv_mu, rwkv_w_rkv, rwkv_w_o, rwkv_w0, rwkv_w1, rwkv_w2, rwkv_a0, rwkv_a1, rwkv_a2, rwkv_v0, rwkv_v1, rwkv_v2, rwkv_g1, rwkv_g2, rwkv_k_k, rwkv_k_a, rwkv_r_k, rwkv_lnx_w, rwkv_lnx_b, swa_w_qkv, swa_b_qkv, swa_w_o, swa_b_o, swa_sinks, conv_w_in, conv_w, conv_w_out, mlp_w_up, mlp_w_down):
    bsz, seq_len, d = x.shape
    depth = norm_mix.shape[0]
    xf = x.reshape(bsz * seq_len, d)
    rwkv_w_rkv, rwkv_w_o, swa_w_o, conv_w_in, conv_w_out, mlp_w_up, mlp_w_down = (
        t.astype(BF16) for t in (rwkv_w_rkv, rwkv_w_o, swa_w_o, conv_w_in, conv_w_out, mlp_w_up, mlp_w_down))
    v_first = None
    ia = ib = ic = 0
    for i in range(depth):
        kind = i % 3
        if kind == 0:
            vres = None if ia == 0 else (rwkv_v0[ia - 1], rwkv_v1[ia - 1], rwkv_v2[ia - 1])
            xf, v_first = _rwkv_layer(
                xf, seq_len, norm_mix[i], rwkv_mu[ia], rwkv_w_rkv, rwkv_w_o, ia,
                rwkv_w0[ia], rwkv_w1[ia], rwkv_w2[ia], rwkv_a0[ia], rwkv_a1[ia], rwkv_a2[ia],
                rwkv_g1[ia], rwkv_g2[ia], rwkv_k_k[ia], rwkv_k_a[ia], rwkv_r_k[ia],
                rwkv_lnx_w[ia], rwkv_lnx_b[ia], v_first, vres)
            ia += 1
        elif kind == 1:
            xf = _swa_layer(xf, seq_len, norm_mix[i], swa_w_qkv[ib], swa_b_qkv[ib],
                            swa_w_o, ib, swa_b_o[ib], swa_sinks[ib])
            ib += 1
        else:
            xf = _conv_layer(xf, seq_len, norm_mix[i], conv_w_in, conv_w, conv_w_out, ic)
            ic += 1
        xf = _ffn(xf, norm_ffn[i], mlp_w_up, mlp_w_down, i, norm_final if i == depth - 1 else None)
    return xf.reshape(bsz, seq_len, d)
```

```python
import functools

import jax
import jax.numpy as jnp
from jax import lax
from jax.experimental import pallas as pl
from jax.experimental.pallas import tpu as pltpu

F32 = jnp.float32
BF16 = jnp.bfloat16

HEAD = 64
LANES = 128
SUBLANES = 8
CHUNK = 64
ATT_BLOCK = 128
RMS_EPS = 1e-6
GN_EPS = 64e-5
DECAY_SCALE = 0.6065306597126334
VMEM_LIMIT = 56 * 1024 * 1024

_NT = (((1,), (1,)), ((), ()))
_TN = (((0,), (0,)), ((), ()))


def _tile(n, pref):
    t = min(n, pref)
    while n % t:
        t //= 2
    return t


def _params(sem):
    return pltpu.CompilerParams(dimension_semantics=sem, vmem_limit_bytes=VMEM_LIMIT)


def _rms(x, g):
    ms = jnp.mean(x * x, axis=-1, keepdims=True)
    return x * lax.rsqrt(ms + RMS_EPS) * g


def _dot(a, b):
    return jnp.dot(a, b, preferred_element_type=F32)


def _split_hl(x):
    hi = x.astype(BF16)
    lo = (x - hi.astype(F32)).astype(BF16)
    return jnp.concatenate([hi, lo], axis=1)


def _mm_kernel(*refs, has_norm, has_bias, has_res):
    it = iter(refs)
    x_ref = next(it)
    g_ref = next(it) if has_norm else None
    w_ref = next(it)
    b_ref = next(it) if has_bias else None
    r_ref = next(it) if has_res else None
    o_ref = next(it)
    xn_ref = next(it) if has_norm else None

    if has_norm:
        @pl.when(pl.program_id(1) == 0)
        def _():
            xn_ref[...] = _rms(x_ref[...], g_ref[...]).astype(BF16)
        xb = xn_ref[...]
    else:
        xb = x_ref[...]
    acc = _dot(xb, w_ref[...])
    if has_bias:
        acc = acc + b_ref[...]
    if has_res:
        acc = acc + r_ref[...]
    o_ref[...] = acc.astype(o_ref.dtype)


def _mm(x, w, *, w_lead=(), norm_g=None, bias=None, res=None, out_dtype=F32, tm=512, tn=4096, name="mm"):
    m, k = x.shape
    n = w.shape[-1]
    squeezed = (None,) * len(w_lead)
    tm, tn = _tile(m, tm), _tile(n, tn)
    has_norm, has_bias, has_res = norm_g is not None, bias is not None, res is not None
    args, specs = [x], [pl.BlockSpec((tm, k), lambda i, j: (i, 0))]
    if has_norm:
        args.append(norm_g.reshape(1, k))
        specs.append(pl.BlockSpec((1, k), lambda i, j: (0, 0)))
    args.append(w)
    specs.append(pl.BlockSpec(squeezed + (k, tn), lambda i, j: w_lead + (0, j)))
    if has_bias:
        args.append(bias.reshape(1, n))
        specs.append(pl.BlockSpec((1, tn), lambda i, j: (0, j)))
    if has_res:
        args.append(res)
        specs.append(pl.BlockSpec((tm, tn), lambda i, j: (i, j)))
    return pl.pallas_call(
        functools.partial(_mm_kernel, has_norm=has_norm, has_bias=has_bias, has_res=has_res),
        out_shape=jax.ShapeDtypeStruct((m, n), out_dtype),
        grid=(m // tm, n // tn),
        in_specs=specs,
        out_specs=pl.BlockSpec((tm, tn), lambda i, j: (i, j)),
        scratch_shapes=[pltpu.VMEM((tm, k), BF16)] if has_norm else [],
        compiler_params=_params(("parallel", "arbitrary")),
        name=name,
    )(*args)


def _ffn_kernel(*refs, has_final):
    if has_final:
        x_ref, g_ref, wu_ref, wd_ref, gf_ref, o_ref, xn_ref = refs
    else:
        x_ref, g_ref, wu_ref, wd_ref, o_ref, xn_ref = refs
    f = pl.program_id(1)

    @pl.when(f == 0)
    def _():
        x = x_ref[...]
        xn_ref[...] = _rms(x, g_ref[...]).astype(BF16)
        o_ref[...] = x

    h = jnp.maximum(_dot(xn_ref[...], wu_ref[...]), 0.0)
    o_ref[...] += _dot((h * h).astype(BF16), wd_ref[...])

    if has_final:
        @pl.when(f == pl.num_programs(1) - 1)
        def _():
            o_ref[...] = _rms(o_ref[...], gf_ref[...])


def _ffn(x, g, w_up, w_down, layer, final_g=None, *, tm=512, tf=1024):
    m, d = x.shape
    ff = w_up.shape[-1]
    tm, tf = _tile(m, tm), _tile(ff, tf)
    has_final = final_g is not None
    args = [x, g.reshape(1, d), w_up, w_down]
    specs = [
        pl.BlockSpec((tm, d), lambda i, f: (i, 0)),
        pl.BlockSpec((1, d), lambda i, f: (0, 0)),
        pl.BlockSpec((None, d, tf), lambda i, f: (layer, 0, f)),
        pl.BlockSpec((None, tf, d), lambda i, f: (layer, f, 0)),
    ]
    if has_final:
        args.append(final_g.reshape(1, d))
        specs.append(pl.BlockSpec((1, d), lambda i, f: (0, 0)))
    return pl.pallas_call(
        functools.partial(_ffn_kernel, has_final=has_final),
        out_shape=jax.ShapeDtypeStruct((m, d), F32),
        grid=(m // tm, ff // tf),
        in_specs=specs,
        out_specs=pl.BlockSpec((tm, d), lambda i, f: (i, 0)),
        scratch_shapes=[pltpu.VMEM((tm, d), BF16)],
        compiler_params=_params(("parallel", "arbitrary")),
        name="ffn",
    )(*args)


def _mix_kernel(*refs, tiles_per_seq, has_vres, ranks):
    it = iter(refs)
    x_ref, p_ref, g_ref, mu_ref, a_ref = (next(it) for _ in range(5))
    w2_ref, a2_ref, g2_ref = next(it), next(it), next(it)
    v2_ref = next(it) if has_vres else None
    w0_ref, a0_ref = next(it), next(it)
    v0_ref = next(it) if has_vres else None
    xr_ref, xk_ref, xv_ref, w_ref, ao_ref, go_ref = (next(it) for _ in range(6))
    vg_ref = next(it) if has_vres else None

    g = g_ref[...]
    h = _rms(x_ref[...], g)
    hp = _rms(p_ref[...], g)[SUBLANES - 1:SUBLANES, :]
    hp = jnp.where(pl.program_id(0) % tiles_per_seq == 0, 0.0, hp)
    row = lax.broadcasted_iota(jnp.int32, h.shape, 0)
    shifted = jnp.where(row == 0, hp, pltpu.roll(h, 1, 0))
    xx = shifted - h

    def mix(c):
        return (h + xx * mu_ref[c:c + 1, :]).astype(BF16)

    rw, ra, rg, rv = ranks
    xr_ref[...] = mix(0)
    xk_ref[...] = mix(2)
    xv = mix(3)
    xv_ref[...] = xv
    mid = jnp.tanh(_dot(mix(1), a_ref[:, :rw])).astype(BF16)
    w_ref[...] = w0_ref[...] + _dot(mid, w2_ref[...])
    mid = _dot(mix(4), a_ref[:, rw:rw + ra]).astype(BF16)
    ao_ref[...] = jax.nn.sigmoid(a0_ref[...] + _dot(mid, a2_ref[...]))
    mid = jax.nn.sigmoid(_dot(mix(5), a_ref[:, rw + ra:rw + ra + rg])).astype(BF16)
    go_ref[...] = _dot(mid, g2_ref[...]).astype(BF16)
    if has_vres:
        mid = _dot(xv, a_ref[:, rw + ra + rg:]).astype(BF16)
        vg_ref[...] = jax.nn.sigmoid(v0_ref[...] + _dot(mid, v2_ref[...]))


def _pad_rank(a, b):
    pad = -a.shape[1] % LANES
    return jnp.pad(a, ((0, 0), (0, pad))).astype(BF16), jnp.pad(b, ((0, pad), (0, 0))).astype(BF16)


def _rwkv_mix(x, g, mu, seq_len, w0, w1, w2, a0, a1, a2, g1, g2, vres, *, tm=256):
    m, d = x.shape
    tm = _tile(seq_len, tm)
    blk = tm // SUBLANES
    has_vres = vres is not None
    pairs = [_pad_rank(w1, w2), _pad_rank(a1, a2), _pad_rank(g1, g2)]
    biases = [w0, a0]
    if has_vres:
        pairs.append(_pad_rank(vres[1], vres[2]))
        biases.append(vres[0])
    first = jnp.concatenate([p[0] for p in pairs], axis=1)
    ranks = tuple(p[0].shape[1] for p in pairs) + ((0,) if not has_vres else ())
    row_spec = pl.BlockSpec((tm, d), lambda i: (i, 0))
    full = lambda arr: pl.BlockSpec(arr.shape, lambda i: (0, 0))
    seconds = [p[1] for p in pairs]
    biases = [b.reshape(1, d) for b in biases]
    out_dtypes = [BF16, BF16, BF16, F32, F32, BF16] + ([F32] if has_vres else [])
    outs = pl.pallas_call(
        functools.partial(_mix_kernel, tiles_per_seq=seq_len // tm, has_vres=has_vres, ranks=ranks),
        out_shape=[jax.ShapeDtypeStruct((m, d), dt) for dt in out_dtypes],
        grid=(m // tm,),
        in_specs=[
            row_spec,
            pl.BlockSpec((SUBLANES, d), lambda i: (jnp.maximum(i * blk - 1, 0), 0)),
            pl.BlockSpec((1, d), lambda i: (0, 0)),
            full(mu), full(first),
        ] + [full(s) for s in seconds] + [full(b) for b in biases],
        out_specs=[row_spec] * len(out_dtypes),
        compiler_params=_params(("parallel",)),
        name="rwkv_mix",
    )(x, x, g.reshape(1, d), mu, first, *seconds, *biases)
    return outs


def _scan_kernel(*refs, has_vres, n_chunks, n_pairs):
    if has_vres:
        (r_ref, k_ref, v_ref, w_ref, a_ref, g_ref, vf_ref, vg_ref,
         kk_ref, ka_ref, rk_ref, lw_ref, lb_ref, o_ref, ht_ref) = refs
    else:
        (r_ref, k_ref, v_ref, w_ref, a_ref, g_ref,
         kk_ref, ka_ref, rk_ref, lw_ref, lb_ref, o_ref, ht_ref) = refs
    c2 = 2 * CHUNK

    @pl.when(pl.program_id(2) == 0)
    def _():
        ht_ref[...] = jnp.zeros_like(ht_ref)

    lane = lax.broadcasted_iota(jnp.int32, (CHUNK, LANES), 1)
    head0 = lane < HEAD

    def both(x, y):
        return jnp.concatenate([x, y], axis=0)

    def split(x0, x1):
        return jnp.where(head0, x0, x1)

    ri = lax.broadcasted_iota(jnp.int32, (LANES, LANES), 0)
    ci = lax.broadcasted_iota(jnp.int32, (LANES, LANES), 1)

    def head_sum(x):
        s0 = jnp.sum(jnp.where(head0, x, 0.0), axis=-1, keepdims=True)
        s1 = jnp.sum(jnp.where(head0, 0.0, x), axis=-1, keepdims=True)
        return jnp.where(head0, s0, s1)

    eye = (ri == ci).astype(F32)
    same_head = (ri // HEAD) == (ci // HEAD)
    ti = lax.broadcasted_iota(jnp.int32, (CHUNK, CHUNK), 0)
    si = lax.broadcasted_iota(jnp.int32, (CHUNK, CHUNK), 1)
    tri = (ti >= si).astype(BF16)
    ts = lax.broadcasted_iota(jnp.int32, (CHUNK, LANES), 0)
    strict = ts > lane % HEAD
    incl = ts >= lane % HEAD

    ops, ys = {}, {}

    def head(p, c):
        rows = slice(c * CHUNK, (c + 1) * CHUNK)
        cols = slice(p * LANES, (p + 1) * LANES)
        r, k, v, a = r_ref[rows, cols], k_ref[rows, cols], v_ref[rows, cols], a_ref[rows, cols]
        k_k, k_a, r_k = kk_ref[:, cols], ka_ref[:, cols], rk_ref[:, cols]
        if has_vres:
            v = v + (vf_ref[rows, cols] - v) * vg_ref[rows, cols]
        lw = -DECAY_SCALE * jax.nn.sigmoid(w_ref[rows, cols])
        lp = _dot(tri, _split_hl(lw))
        kk = k * k_k
        norm = jnp.sqrt(head_sum(kk * kk))
        yield
        kmod = k * (1.0 + (a - 1.0) * k_a)
        bonus = head_sum(r * kmod * r_k)
        yield
        lp = lp[:, :LANES] + lp[:, LANES:]
        lp_end = lp[CHUNK - 1:CHUNK, :]
        en = jnp.exp(-lp)
        to_end = jnp.exp(lp_end - lp)
        r_t = r * jnp.exp(lp)
        yield
        kk = kk / jnp.maximum(norm, 1e-12)
        bb = kk * a
        a_t = -kk * jnp.exp(lp - lw)
        b_t = (bb * en).astype(BF16)
        k_t = (kmod * en).astype(BF16)
        v_b = v.astype(BF16)
        yield
        ops[p, c] = dict(
            lhs0=both(jnp.where(head0, a_t, 0.0), jnp.where(head0, r_t, 0.0)).astype(BF16),
            lhs1=both(jnp.where(head0, 0.0, a_t), jnp.where(head0, 0.0, r_t)).astype(BF16),
            rhs0=both(b_t, k_t), rhs1=both(k_t, b_t),
            ar=both(a_t, r_t).astype(BF16), v_b=v_b,
            bk_end=both((bb * to_end).astype(BF16), (kmod * to_end).astype(BF16)),
            decay_end=jnp.exp(lp_end), bonus_v=bonus * v)
        yield

    def mid(p, c):
        o = ops.pop((p, c))
        v_b = o["v_b"]
        g0 = lax.dot_general(o["lhs0"], o["rhs0"], _NT, preferred_element_type=F32)
        g1 = lax.dot_general(o["lhs1"], o["rhs1"], _NT, preferred_element_type=F32)
        yield
        top0 = jnp.where(strict, g0[:CHUNK, :], 0.0)
        top1 = jnp.where(strict, g1[:CHUNK, :], 0.0)
        bot0 = jnp.where(incl, g0[CHUNK:, :], 0.0).astype(BF16)
        bot1 = jnp.where(incl, g1[CHUNK:, :], 0.0).astype(BF16)
        n_ab = both(jnp.where(head0, top0, 0.0), jnp.where(head0, 0.0, top1))
        n_ak = both(jnp.where(head0, 0.0, top0), jnp.where(head0, top1, 0.0)).astype(BF16)

        inv = eye + n_ab
        pw = n_ab.astype(BF16)
        pw = _dot(pw, pw).astype(BF16)
        yield
        for _ in range(CHUNK.bit_length() - 3):
            prod = _dot(pw, jnp.concatenate([pw, inv.astype(BF16)], axis=1))
            yield
            pw = prod[:, :c2].astype(BF16)
            inv = inv + prod[:, c2:]
        inv = inv + _dot(pw, inv.astype(BF16))

        ht = ht_ref[p]
        state_proj = lax.dot_general(o["ar"], ht.astype(BF16), _NT, preferred_element_type=F32)
        xv = _dot(n_ak, both(v_b, v_b))
        yield
        sp_a = state_proj[:CHUNK, :]
        xs = both(jnp.where(head0, xv[:CHUNK, :] + sp_a, 0.0), jnp.where(head0, 0.0, xv[CHUNK:, :] + sp_a))
        us = _dot(inv.astype(BF16), xs.astype(BF16))
        yield
        u_b = (us[:CHUNK, :] + us[CHUNK:, :]).astype(BF16)
        y = state_proj[CHUNK:, :] + split(_dot(bot0, both(u_b, v_b)), _dot(bot1, both(v_b, u_b)))
        new = lax.dot_general(both(u_b, v_b), o["bk_end"], _TN, preferred_element_type=F32)
        ht_ref[p] = ht * o["decay_end"] + jnp.where(same_head, new, 0.0)
        yield
        ys[p, c] = (y, o["bonus_v"])
        yield

    def tail(p, c):
        rows = slice(c * CHUNK, (c + 1) * CHUNK)
        cols = slice(p * LANES, (p + 1) * LANES)
        y, bonus_v = ys.pop((p, c))
        mean = head_sum(y) * (1.0 / HEAD)
        yield
        yc = y - mean
        var = head_sum(yc * yc) * (1.0 / HEAD)
        yield
        out = yc * lax.rsqrt(var + GN_EPS) * lw_ref[:, cols] + lb_ref[:, cols] + bonus_v
        o_ref[rows, cols] = (out * g_ref[rows, cols]).astype(o_ref.dtype)
        yield

    done = object()

    def step_all(gens):
        return any([next(g, done) is not done for g in gens])

    pairs = range(n_pairs)
    heads = [head(p, 0) for p in pairs]
    while step_all(heads):
        pass
    for c in range(n_chunks):
        mids = [mid(p, c) for p in pairs]
        heads = [head(p, c + 1) for p in pairs] if c + 1 < n_chunks else []
        tails = [tail(p, c - 1) for p in pairs] if c > 0 else []
        step, busy = 0, True
        while busy:
            busy = False
            for p in pairs:
                busy = (next(mids[p], done) is not done) or busy
                side = (heads if step % 2 == 0 else tails) or heads or tails
                if side:
                    busy = (next(side[p], done) is not done) or busy
            step += 1
    tails = [tail(p, n_chunks - 1) for p in pairs]
    while step_all(tails):
        pass


def _rwkv_scan(r, k, v, w, a, g, k_k, k_a, r_k, ln_w, ln_b, seq_len, vres=None, *, tc=256, n_pairs=16):
    m, d = r.shape
    tc = _tile(seq_len, tc)
    nt = seq_len // tc
    n_pairs = _tile(d // LANES, n_pairs)
    width = n_pairs * LANES
    row_spec = pl.BlockSpec((tc, width), lambda b, h, c: (b * nt + c, h))
    par_spec = pl.BlockSpec((1, width), lambda b, h, c: (0, h))
    seqs = [r, k, v, w, a, g] + (list(vres) if vres is not None else [])
    pars = [p.reshape(1, d) for p in (k_k, k_a, r_k, ln_w, ln_b)]
    return pl.pallas_call(
        functools.partial(_scan_kernel, has_vres=vres is not None, n_chunks=tc // CHUNK, n_pairs=n_pairs),
        out_shape=jax.ShapeDtypeStruct((m, d), BF16),
        grid=(m // seq_len, d // width, nt),
        in_specs=[row_spec] * len(seqs) + [par_spec] * len(pars),
        out_specs=row_spec,
        scratch_shapes=[pltpu.VMEM((n_pairs, LANES, LANES), F32)],
        compiler_params=_params(("parallel", "parallel", "arbitrary")),
        name="rwkv_scan",
    )(*seqs, *pars)


def _swa_kernel(sink_ref, q_ref, kp_ref, kc_ref, vp_ref, vc_ref, o_ref, *, n_kv, q_per_kv, blocks_per_seq):
    first = pl.program_id(0) % blocks_per_seq == 0
    lane = lax.broadcasted_iota(jnp.int32, (ATT_BLOCK, LANES), 1)
    even = lane < HEAD
    qi = lax.broadcasted_iota(jnp.int32, (ATT_BLOCK, 2 * ATT_BLOCK), 0)
    kc = lax.broadcasted_iota(jnp.int32, (ATT_BLOCK, 2 * ATT_BLOCK), 1)
    valid = (kc > qi) & (kc <= qi + ATT_BLOCK) & ((kc >= ATT_BLOCK) | jnp.logical_not(first))
    for gi in range(n_kv):
        cols = slice(gi * LANES, (gi + 1) * LANES)
        keys = jnp.concatenate([kp_ref[:, cols], kc_ref[:, cols]], axis=0)
        vals = jnp.concatenate([vp_ref[:, cols], vc_ref[:, cols]], axis=0)
        for pi in range(q_per_kv // 2):
            tile = gi * (q_per_kv // 2) + pi
            q2 = q_ref[:, tile * LANES:(tile + 1) * LANES]
            outs = []
            for half in range(2):
                sink = sink_ref[2 * tile + half]
                qh = jnp.where(even if half == 0 else jnp.logical_not(even), q2, jnp.zeros_like(q2))
                logits = lax.dot_general(qh, keys, _NT, preferred_element_type=F32) * (HEAD ** -0.5)
                logits = jnp.where(valid, logits, -jnp.inf)
                mx = jnp.maximum(jnp.max(logits, axis=-1, keepdims=True), sink)
                p = jnp.exp(logits - mx)
                denom = jnp.sum(p, axis=-1, keepdims=True) + jnp.exp(sink - mx)
                outs.append(_dot((p / denom).astype(BF16), vals))
            o_ref[:, tile * LANES:(tile + 1) * LANES] = jnp.where(even, outs[0], outs[1]).astype(o_ref.dtype)


def _swa(qkv, sinks, seq_len, d, n_kv):
    m = qkv.shape[0]
    n_q = d // HEAD
    q_tiles = d // LANES
    blocks_per_seq = seq_len // ATT_BLOCK
    kw = n_kv * LANES
    k_blk = d // kw
    prev = lambda i: jnp.maximum(i - 1, 0)
    return pl.pallas_call(
        functools.partial(_swa_kernel, n_kv=n_kv, q_per_kv=n_q // n_kv, blocks_per_seq=blocks_per_seq),
        out_shape=jax.ShapeDtypeStruct((m, d), BF16),
        grid=(m // ATT_BLOCK,),
        in_specs=[
            pl.BlockSpec(memory_space=pltpu.SMEM),
            pl.BlockSpec((ATT_BLOCK, d), lambda i: (i, 0)),
            pl.BlockSpec((ATT_BLOCK, kw), lambda i: (prev(i), k_blk)),
            pl.BlockSpec((ATT_BLOCK, kw), lambda i: (i, k_blk)),
            pl.BlockSpec((ATT_BLOCK, kw), lambda i: (prev(i), k_blk + 1)),
            pl.BlockSpec((ATT_BLOCK, kw), lambda i: (i, k_blk + 1)),
        ],
        out_specs=pl.BlockSpec((ATT_BLOCK, d), lambda i: (i, 0)),
        compiler_params=_params(("parallel",)),
        name="swa",
    )(sinks, qkv, qkv, qkv, qkv, qkv)


def _conv_kernel(x_ref, p_ref, g_ref, wb_ref, wc_ref, wh_ref, cw_ref, o_ref, xn_ref, *, tiles_per_seq):
    @pl.when(pl.program_id(1) == 0)
    def _():
        g = g_ref[...]
        xn_ref[:SUBLANES, :] = _rms(p_ref[...], g).astype(BF16)
        xn_ref[SUBLANES:, :] = _rms(x_ref[...], g).astype(BF16)

    xn = xn_ref[...]
    u = _dot(xn, wc_ref[...]) * _dot(xn, wh_ref[...])
    row = lax.broadcasted_iota(jnp.int32, u.shape, 0)
    first = pl.program_id(0) % tiles_per_seq == 0
    u = jnp.where((row < SUBLANES) & first, 0.0, u)
    cw = cw_ref[...]
    uc = (cw[2:3, :] * u + cw[1:2, :] * pltpu.roll(u, 1, 0) + cw[0:1, :] * pltpu.roll(u, 2, 0))[SUBLANES:, :]
    bg = _dot(xn[SUBLANES:, :], wb_ref[...])
    o_ref[...] = (bg * uc).astype(o_ref.dtype)


def _conv_front(x, g, w_in, conv_w, layer, seq_len, *, tm=512, tn=512):
    m, d = x.shape
    tm, tn = _tile(seq_len, tm), _tile(d, tn)
    nj = d // tn
    blk = tm // SUBLANES
    return pl.pallas_call(
        functools.partial(_conv_kernel, tiles_per_seq=seq_len // tm),
        out_shape=jax.ShapeDtypeStruct((m, d), BF16),
        grid=(m // tm, nj),
        in_specs=[
            pl.BlockSpec((tm, d), lambda i, j: (i, 0)),
            pl.BlockSpec((SUBLANES, d), lambda i, j: (jnp.maximum(i * blk - 1, 0), 0)),
            pl.BlockSpec((1, d), lambda i, j: (0, 0)),
            pl.BlockSpec((None, d, tn), lambda i, j: (layer, 0, j)),
            pl.BlockSpec((None, d, tn), lambda i, j: (layer, 0, nj + j)),
            pl.BlockSpec((None, d, tn), lambda i, j: (layer, 0, 2 * nj + j)),
            pl.BlockSpec((None, conv_w.shape[1], tn), lambda i, j: (layer, 0, j)),
        ],
        out_specs=pl.BlockSpec((tm, tn), lambda i, j: (i, j)),
        scratch_shapes=[pltpu.VMEM((SUBLANES + tm, d), BF16)],
        compiler_params=_params(("parallel", "arbitrary")),
        name="conv_front",
    )(x, x, g.reshape(1, d), w_in, w_in, w_in, conv_w)


def _rwkv_layer(x, seq_len, g_norm, mu, w_rkv, w_o, layer, w0, w1, w2, a0, a1, a2, g1, g2,
                k_k, k_a, r_k, ln_w, ln_b, v_first, vres):
    outs = _rwkv_mix(x, g_norm, mu, seq_len, w0, w1, w2, a0, a1, a2, g1, g2, vres)
    xr, xk, xv, w, a, g = outs[:6]
    r = _mm(xr, w_rkv, w_lead=(layer, 0), name="rwkv_r")
    k = _mm(xk, w_rkv, w_lead=(layer, 1), name="rwkv_k")
    v = _mm(xv, w_rkv, w_lead=(layer, 2), name="rwkv_v")
    if vres is None:
        scan_vres = None
        v_first = v
    else:
        scan_vres = (v_first, outs[6])
    yg = _rwkv_scan(r, k, v, w, a, g, k_k, k_a, r_k.reshape(-1), ln_w, ln_b, seq_len, scan_vres)
    return _mm(yg, w_o, w_lead=(layer,), res=x, name="rwkv_o"), v_first


def _swa_layer(x, seq_len, g_norm, w_qkv, b_qkv, w_o, layer, b_o, sinks):
    d = x.shape[1]
    n_kv = (w_qkv.shape[1] - d) // (2 * HEAD)
    kvd = n_kv * HEAD

    def dup(t):
        lead = t.shape[:-1]
        t = t.reshape(lead + (n_kv, 1, HEAD))
        return jnp.broadcast_to(t, lead + (n_kv, 2, HEAD)).reshape(lead + (2 * kvd,))

    w_cat = jnp.concatenate([w_qkv[:, :d], dup(w_qkv[:, d:d + kvd]), dup(w_qkv[:, d + kvd:])], axis=1)
    b_cat = jnp.concatenate([b_qkv[:d], dup(b_qkv[d:d + kvd]), dup(b_qkv[d + kvd:])])
    qkv = _mm(x, w_cat.astype(BF16), norm_g=g_norm, bias=b_cat, out_dtype=BF16, name="swa_qkv")
    o = _swa(qkv, sinks, seq_len, d, n_kv)
    return _mm(o, w_o, w_lead=(layer,), bias=b_o, res=x, name="swa_o")


def _conv_layer(x, seq_len, g_norm, w_in, conv_w, w_out, layer):
    gated = _conv_front(x, g_norm, w_in, conv_w, layer, seq_len)
    return _mm(gated, w_out, w_lead=(layer,), res=x, name="conv_o")


def kernel(x, norm_mix, norm_ffn, norm_final, rwkv_mu, rwkv_w_rkv, rwkv_w_o, rwkv_w0, rwkv_w1, rwkv_w2, rwkv_a0, rwkv_a1, rwkv_a2, rwkv_v0, rwkv_v1, rwkv_v2, rwkv_g1, rwkv_g2, rwkv_k_k, rwkv_k_a, rwkv_r_k, rwkv_lnx_w, rwkv_lnx_b, swa_w_qkv, swa_b_qkv, swa_w_o, swa_b_o, swa_sinks, conv_w_in, conv_w, conv_w_out, mlp_w_up, mlp_w_down):
    bsz, seq_len, d = x.shape
    depth = norm_mix.shape[0]
    xf = x.reshape(bsz * seq_len, d)
    rwkv_w_rkv, rwkv_w_o, swa_w_o, conv_w_in, conv_w_out, mlp_w_up, mlp_w_down = (
        t.astype(BF16) for t in (rwkv_w_rkv, rwkv_w_o, swa_w_o, conv_w_in, conv_w_out, mlp_w_up, mlp_w_down))
    v_first = None
    ia = ib = ic = 0
    for i in range(depth):
        kind = i % 3
        if kind == 0:
            vres = None if ia == 0 else (rwkv_v0[ia - 1], rwkv_v1[ia - 1], rwkv_v2[ia - 1])
            xf, v_first = _rwkv_layer(
                xf, seq_len, norm_mix[i], rwkv_mu[ia], rwkv_w_rkv, rwkv_w_o, ia,
                rwkv_w0[ia], rwkv_w1[ia], rwkv_w2[ia], rwkv_a0[ia], rwkv_a1[ia], rwkv_a2[ia],
                rwkv_g1[ia], rwkv_g2[ia], rwkv_k_k[ia], rwkv_k_a[ia], rwkv_r_k[ia],
                rwkv_lnx_w[ia], rwkv_lnx_b[ia], v_first, vres)
            ia += 1
        elif kind == 1:
            xf = _swa_layer(xf, seq_len, norm_mix[i], swa_w_qkv[ib], swa_b_qkv[ib],
                            swa_w_o, ib, swa_b_o[ib], swa_sinks[ib])
            ib += 1
        else:
            xf = _conv_layer(xf, seq_len, norm_mix[i], conv_w_in, conv_w, conv_w_out, ic)
            ic += 1
        xf = _ffn(xf, norm_ffn[i], mlp_w_up, mlp_w_down, i, norm_final if i == depth - 1 else None)
    return xf.reshape(bsz, seq_len, d)
```

```python
import functools

import jax
import jax.numpy as jnp
from jax import lax
from jax.experimental import pallas as pl
from jax.experimental.pallas import tpu as pltpu

F32 = jnp.float32
BF16 = jnp.bfloat16

HEAD = 64
LANES = 128
SUBLANES = 8
CHUNK = 64
ATT_BLOCK = 128
STAGGER = 6
RMS_EPS = 1e-6
GN_EPS = 64e-5
DECAY_SCALE = 0.6065306597126334
VMEM_LIMIT = 56 * 1024 * 1024

_NT = (((1,), (1,)), ((), ()))
_TN = (((0,), (0,)), ((), ()))


def _tile(n, pref):
    t = min(n, pref)
    while n % t:
        t //= 2
    return t


def _params(sem):
    return pltpu.CompilerParams(dimension_semantics=sem, vmem_limit_bytes=VMEM_LIMIT)


def _rms(x, g):
    ms = jnp.mean(x * x, axis=-1, keepdims=True)
    return x * lax.rsqrt(ms + RMS_EPS) * g


def _dot(a, b):
    return jnp.dot(a, b, preferred_element_type=F32)


def _split_hl(x):
    hi = x.astype(BF16)
    lo = (x - hi.astype(F32)).astype(BF16)
    return jnp.concatenate([hi, lo], axis=1)


def _mm_kernel(*refs, has_norm, has_bias, has_res):
    it = iter(refs)
    x_ref = next(it)
    g_ref = next(it) if has_norm else None
    w_ref = next(it)
    b_ref = next(it) if has_bias else None
    r_ref = next(it) if has_res else None
    o_ref = next(it)
    xn_ref = next(it) if has_norm else None

    if has_norm:
        @pl.when(pl.program_id(1) == 0)
        def _():
            xn_ref[...] = _rms(x_ref[...], g_ref[...]).astype(BF16)
        xb = xn_ref[...]
    else:
        xb = x_ref[...]
    acc = _dot(xb, w_ref[...])
    if has_bias:
        acc = acc + b_ref[...]
    if has_res:
        acc = acc + r_ref[...]
    o_ref[...] = acc.astype(o_ref.dtype)


def _mm(x, w, *, w_lead=(), norm_g=None, bias=None, res=None, out_dtype=F32, tm=512, tn=4096, name="mm"):
    m, k = x.shape
    n = w.shape[-1]
    squeezed = (None,) * len(w_lead)
    tm, tn = _tile(m, tm), _tile(n, tn)
    has_norm, has_bias, has_res = norm_g is not None, bias is not None, res is not None
    args, specs = [x], [pl.BlockSpec((tm, k), lambda i, j: (i, 0))]
    if has_norm:
        args.append(norm_g.reshape(1, k))
        specs.append(pl.BlockSpec((1, k), lambda i, j: (0, 0)))
    args.append(w)
    specs.append(pl.BlockSpec(squeezed + (k, tn), lambda i, j: w_lead + (0, j)))
    if has_bias:
        args.append(bias.reshape(1, n))
        specs.append(pl.BlockSpec((1, tn), lambda i, j: (0, j)))
    if has_res:
        args.append(res)
        specs.append(pl.BlockSpec((tm, tn), lambda i, j: (i, j)))
    return pl.pallas_call(
        functools.partial(_mm_kernel, has_norm=has_norm, has_bias=has_bias, has_res=has_res),
        out_shape=jax.ShapeDtypeStruct((m, n), out_dtype),
        grid=(m // tm, n // tn),
        in_specs=specs,
        out_specs=pl.BlockSpec((tm, tn), lambda i, j: (i, j)),
        scratch_shapes=[pltpu.VMEM((tm, k), BF16)] if has_norm else [],
        compiler_params=_params(("parallel", "arbitrary")),
        name=name,
    )(*args)


def _ffn_kernel(*refs, has_final):
    if has_final:
        x_ref, g_ref, wu_ref, wd_ref, gf_ref, o_ref, xn_ref = refs
    else:
        x_ref, g_ref, wu_ref, wd_ref, o_ref, xn_ref = refs
    f = pl.program_id(1)

    @pl.when(f == 0)
    def _():
        x = x_ref[...]
        xn_ref[...] = _rms(x, g_ref[...]).astype(BF16)
        o_ref[...] = x

    h = jnp.maximum(_dot(xn_ref[...], wu_ref[...]), 0.0)
    o_ref[...] += _dot((h * h).astype(BF16), wd_ref[...])

    if has_final:
        @pl.when(f == pl.num_programs(1) - 1)
        def _():
            o_ref[...] = _rms(o_ref[...], gf_ref[...])


def _ffn(x, g, w_up, w_down, layer, final_g=None, *, tm=512, tf=1024):
    m, d = x.shape
    ff = w_up.shape[-1]
    tm, tf = _tile(m, tm), _tile(ff, tf)
    has_final = final_g is not None
    args = [x, g.reshape(1, d), w_up, w_down]
    specs = [
        pl.BlockSpec((tm, d), lambda i, f: (i, 0)),
        pl.BlockSpec((1, d), lambda i, f: (0, 0)),
        pl.BlockSpec((None, d, tf), lambda i, f: (layer, 0, f)),
        pl.BlockSpec((None, tf, d), lambda i, f: (layer, f, 0)),
    ]
    if has_final:
        args.append(final_g.reshape(1, d))
        specs.append(pl.BlockSpec((1, d), lambda i, f: (0, 0)))
    return pl.pallas_call(
        functools.partial(_ffn_kernel, has_final=has_final),
        out_shape=jax.ShapeDtypeStruct((m, d), F32),
        grid=(m // tm, ff // tf),
        in_specs=specs,
        out_specs=pl.BlockSpec((tm, d), lambda i, f: (i, 0)),
        scratch_shapes=[pltpu.VMEM((tm, d), BF16)],
        compiler_params=_params(("parallel", "arbitrary")),
        name="ffn",
    )(*args)


def _mix_kernel(*refs, tiles_per_seq, has_vres, ranks):
    it = iter(refs)
    x_ref, p_ref, g_ref, mu_ref, a_ref = (next(it) for _ in range(5))
    w2_ref, a2_ref, g2_ref = next(it), next(it), next(it)
    v2_ref = next(it) if has_vres else None
    w0_ref, a0_ref = next(it), next(it)
    v0_ref = next(it) if has_vres else None
    xr_ref, xk_ref, xv_ref, w_ref, ao_ref, go_ref = (next(it) for _ in range(6))
    vg_ref = next(it) if has_vres else None

    g = g_ref[...]
    h = _rms(x_ref[...], g)
    hp = _rms(p_ref[...], g)[SUBLANES - 1:SUBLANES, :]
    hp = jnp.where(pl.program_id(0) % tiles_per_seq == 0, 0.0, hp)
    row = lax.broadcasted_iota(jnp.int32, h.shape, 0)
    shifted = jnp.where(row == 0, hp, pltpu.roll(h, 1, 0))
    xx = shifted - h

    def mix(c):
        return (h + xx * mu_ref[c:c + 1, :]).astype(BF16)

    rw, ra, rg, rv = ranks

    def branch(c, lo, hi, mid_act, second_ref, bias_ref, out_act, out_ref, x_out_ref=None):
        xm = mix(c)
        if x_out_ref is not None:
            x_out_ref[...] = xm
        mid = _dot(xm, a_ref[:, lo:hi])
        yield
        out = _dot(mid_act(mid).astype(BF16), second_ref[...])
        yield
        if bias_ref is not None:
            out = out + bias_ref[...]
        out_ref[...] = out_act(out).astype(out_ref.dtype)

    def plain(c, x_out_ref):
        x_out_ref[...] = mix(c)
        yield

    ident = lambda t: t
    branches = [
        branch(1, 0, rw, jnp.tanh, w2_ref, w0_ref, ident, w_ref),
        plain(0, xr_ref),
        branch(4, rw, rw + ra, ident, a2_ref, a0_ref, jax.nn.sigmoid, ao_ref),
        plain(2, xk_ref),
        branch(5, rw + ra, rw + ra + rg, jax.nn.sigmoid, g2_ref, None, ident, go_ref),
    ]
    if has_vres:
        branches.append(branch(3, rw + ra + rg, rw + ra + rg + rv, ident, v2_ref, v0_ref, jax.nn.sigmoid, vg_ref, xv_ref))
    else:
        branches.append(plain(3, xv_ref))
    done = object()
    while any([next(b, done) is not done for b in branches]):
        pass


def _pad_rank(a, b):
    pad = -a.shape[1] % LANES
    return jnp.pad(a, ((0, 0), (0, pad))).astype(BF16), jnp.pad(b, ((0, pad), (0, 0))).astype(BF16)


def _rwkv_mix(x, g, mu, seq_len, w0, w1, w2, a0, a1, a2, g1, g2, vres, *, tm=256):
    m, d = x.shape
    tm = _tile(seq_len, tm)
    blk = tm // SUBLANES
    has_vres = vres is not None
    pairs = [_pad_rank(w1, w2), _pad_rank(a1, a2), _pad_rank(g1, g2)]
    biases = [w0, a0]
    if has_vres:
        pairs.append(_pad_rank(vres[1], vres[2]))
        biases.append(vres[0])
    first = jnp.concatenate([p[0] for p in pairs], axis=1)
    ranks = tuple(p[0].shape[1] for p in pairs) + ((0,) if not has_vres else ())
    row_spec = pl.BlockSpec((tm, d), lambda i: (i, 0))
    full = lambda arr: pl.BlockSpec(arr.shape, lambda i: (0, 0))
    seconds = [p[1] for p in pairs]
    biases = [b.reshape(1, d) for b in biases]
    out_dtypes = [BF16, BF16, BF16, F32, F32, BF16] + ([F32] if has_vres else [])
    outs = pl.pallas_call(
        functools.partial(_mix_kernel, tiles_per_seq=seq_len // tm, has_vres=has_vres, ranks=ranks),
        out_shape=[jax.ShapeDtypeStruct((m, d), dt) for dt in out_dtypes],
        grid=(m // tm,),
        in_specs=[
            row_spec,
            pl.BlockSpec((SUBLANES, d), lambda i: (jnp.maximum(i * blk - 1, 0), 0)),
            pl.BlockSpec((1, d), lambda i: (0, 0)),
            full(mu), full(first),
        ] + [full(s) for s in seconds] + [full(b) for b in biases],
        out_specs=[row_spec] * len(out_dtypes),
        compiler_params=_params(("parallel",)),
        name="rwkv_mix",
    )(x, x, g.reshape(1, d), mu, first, *seconds, *biases)
    return outs


def _scan_kernel(*refs, has_vres, n_chunks, n_pairs):
    if has_vres:
        (r_ref, k_ref, v_ref, w_ref, a_ref, g_ref, vf_ref, vg_ref,
         kk_ref, ka_ref, rk_ref, lw_ref, lb_ref, o_ref, ht_ref) = refs
    else:
        (r_ref, k_ref, v_ref, w_ref, a_ref, g_ref,
         kk_ref, ka_ref, rk_ref, lw_ref, lb_ref, o_ref, ht_ref) = refs
    c2 = 2 * CHUNK

    @pl.when(pl.program_id(2) == 0)
    def _():
        ht_ref[...] = jnp.zeros_like(ht_ref)

    lane = lax.broadcasted_iota(jnp.int32, (CHUNK, LANES), 1)
    head0 = lane < HEAD

    def both(x, y):
        return jnp.concatenate([x, y], axis=0)

    def split(x0, x1):
        return jnp.where(head0, x0, x1)

    ri = lax.broadcasted_iota(jnp.int32, (LANES, LANES), 0)
    ci = lax.broadcasted_iota(jnp.int32, (LANES, LANES), 1)

    def head_sum(x):
        s0 = jnp.sum(jnp.where(head0, x, 0.0), axis=-1, keepdims=True)
        s1 = jnp.sum(jnp.where(head0, 0.0, x), axis=-1, keepdims=True)
        return jnp.where(head0, s0, s1)

    eye = (ri == ci).astype(F32)
    same_head = (ri // HEAD) == (ci // HEAD)
    ti = lax.broadcasted_iota(jnp.int32, (CHUNK, CHUNK), 0)
    si = lax.broadcasted_iota(jnp.int32, (CHUNK, CHUNK), 1)
    tri = (ti >= si).astype(BF16)
    ts = lax.broadcasted_iota(jnp.int32, (CHUNK, LANES), 0)
    strict = ts > lane % HEAD
    incl = ts >= lane % HEAD

    ops, ys = {}, {}

    def head(p, c):
        rows = slice(c * CHUNK, (c + 1) * CHUNK)
        cols = slice(p * LANES, (p + 1) * LANES)
        r, k, v, a = r_ref[rows, cols], k_ref[rows, cols], v_ref[rows, cols], a_ref[rows, cols]
        k_k, k_a, r_k = kk_ref[:, cols], ka_ref[:, cols], rk_ref[:, cols]
        if has_vres:
            v = v + (vf_ref[rows, cols] - v) * vg_ref[rows, cols]
        lw = -DECAY_SCALE * jax.nn.sigmoid(w_ref[rows, cols])
        lp = _dot(tri, _split_hl(lw))
        kk = k * k_k
        norm = jnp.sqrt(head_sum(kk * kk))
        yield
        kmod = k * (1.0 + (a - 1.0) * k_a)
        bonus = head_sum(r * kmod * r_k)
        yield
        lp = lp[:, :LANES] + lp[:, LANES:]
        lp_end = lp[CHUNK - 1:CHUNK, :]
        en = jnp.exp(-lp)
        to_end = jnp.exp(lp_end - lp)
        r_t = r * jnp.exp(lp)
        yield
        kk = kk / jnp.maximum(norm, 1e-12)
        bb = kk * a
        a_t = -kk * jnp.exp(lp - lw)
        b_t = (bb * en).astype(BF16)
        k_t = (kmod * en).astype(BF16)
        v_b = v.astype(BF16)
        yield
        ops[p, c] = dict(
            lhs0=both(jnp.where(head0, a_t, 0.0), jnp.where(head0, r_t, 0.0)).astype(BF16),
            lhs1=both(jnp.where(head0, 0.0, a_t), jnp.where(head0, 0.0, r_t)).astype(BF16),
            rhs0=both(b_t, k_t), rhs1=both(k_t, b_t),
            ar=both(a_t, r_t).astype(BF16), v_b=v_b,
            bk_end=both((bb * to_end).astype(BF16), (kmod * to_end).astype(BF16)),
            decay_end=jnp.exp(lp_end), bonus_v=bonus * v)
        yield

    def mid(p, c):
        o = ops.pop((p, c))
        v_b = o["v_b"]
        g0 = lax.dot_general(o["lhs0"], o["rhs0"], _NT, preferred_element_type=F32)
        g1 = lax.dot_general(o["lhs1"], o["rhs1"], _NT, preferred_element_type=F32)
        yield
        top0 = jnp.where(strict, g0[:CHUNK, :], 0.0)
        top1 = jnp.where(strict, g1[:CHUNK, :], 0.0)
        bot0 = jnp.where(incl, g0[CHUNK:, :], 0.0).astype(BF16)
        bot1 = jnp.where(incl, g1[CHUNK:, :], 0.0).astype(BF16)
        n_ab = both(jnp.where(head0, top0, 0.0), jnp.where(head0, 0.0, top1))
        n_ak = both(jnp.where(head0, 0.0, top0), jnp.where(head0, top1, 0.0)).astype(BF16)

        inv = eye + n_ab
        pw = n_ab.astype(BF16)
        pw = _dot(pw, pw).astype(BF16)
        yield
        for _ in range(CHUNK.bit_length() - 3):
            prod = _dot(pw, jnp.concatenate([pw, inv.astype(BF16)], axis=1))
            yield
            pw = prod[:, :c2].astype(BF16)
            inv = inv + prod[:, c2:]
        inv = inv + _dot(pw, inv.astype(BF16))

        ht = ht_ref[p]
        state_proj = lax.dot_general(o["ar"], ht.astype(BF16), _NT, preferred_element_type=F32)
        xv = _dot(n_ak, both(v_b, v_b))
        yield
        sp_a = state_proj[:CHUNK, :]
        xs = both(jnp.where(head0, xv[:CHUNK, :] + sp_a, 0.0), jnp.where(head0, 0.0, xv[CHUNK:, :] + sp_a))
        us = _dot(inv.astype(BF16), xs.astype(BF16))
        yield
        u_b = (us[:CHUNK, :] + us[CHUNK:, :]).astype(BF16)
        y = state_proj[CHUNK:, :] + split(_dot(bot0, both(u_b, v_b)), _dot(bot1, both(v_b, u_b)))
        new = lax.dot_general(both(u_b, v_b), o["bk_end"], _TN, preferred_element_type=F32)
        ht_ref[p] = ht * o["decay_end"] + jnp.where(same_head, new, 0.0)
        yield
        ys[p, c] = (y, o["bonus_v"])
        yield

    def tail(p, c):
        rows = slice(c * CHUNK, (c + 1) * CHUNK)
        cols = slice(p * LANES, (p + 1) * LANES)
        y, bonus_v = ys.pop((p, c))
        mean = head_sum(y) * (1.0 / HEAD)
        yield
        yc = y - mean
        var = head_sum(yc * yc) * (1.0 / HEAD)
        yield
        out = yc * lax.rsqrt(var + GN_EPS) * lw_ref[:, cols] + lb_ref[:, cols] + bonus_v
        o_ref[rows, cols] = (out * g_ref[rows, cols]).astype(o_ref.dtype)
        yield

    done = object()

    def pipeline(p):
        yield from head(p, 0)
        for c in range(n_chunks):
            main = mid(p, c)
            sides = ([head(p, c + 1)] if c + 1 < n_chunks else []) + ([tail(p, c - 1)] if c > 0 else [])
            step, busy = 0, True
            while busy:
                busy = next(main, done) is not done
                if sides:
                    side = sides[step % len(sides)]
                    if next(side, done) is done:
                        sides.remove(side)
                    else:
                        busy = True
                step += 1
                yield
        yield from tail(p, n_chunks - 1)

    programs = [pipeline(p) for p in range(n_pairs)]
    lag = [0 if p < n_pairs // 2 else STAGGER for p in range(n_pairs)]
    step, live = 0, n_pairs
    while live:
        live = 0
        for p, prog in enumerate(programs):
            if step >= lag[p] and next(prog, done) is not done:
                live += 1
            elif step < lag[p]:
                live += 1
        step += 1


def _rwkv_scan(r, k, v, w, a, g, k_k, k_a, r_k, ln_w, ln_b, seq_len, vres=None, *, tc=256, n_pairs=16):
    m, d = r.shape
    tc = _tile(seq_len, tc)
    nt = seq_len // tc
    n_pairs = _tile(d // LANES, n_pairs)
    width = n_pairs * LANES
    row_spec = pl.BlockSpec((tc, width), lambda b, h, c: (b * nt + c, h))
    par_spec = pl.BlockSpec((1, width), lambda b, h, c: (0, h))
    seqs = [r, k, v, w, a, g] + (list(vres) if vres is not None else [])
    pars = [p.reshape(1, d) for p in (k_k, k_a, r_k, ln_w, ln_b)]
    return pl.pallas_call(
        functools.partial(_scan_kernel, has_vres=vres is not None, n_chunks=tc // CHUNK, n_pairs=n_pairs),
        out_shape=jax.ShapeDtypeStruct((m, d), BF16),
        grid=(m // seq_len, d // width, nt),
        in_specs=[row_spec] * len(seqs) + [par_spec] * len(pars),
        out_specs=row_spec,
        scratch_shapes=[pltpu.VMEM((n_pairs, LANES, LANES), F32)],
        compiler_params=_params(("parallel", "parallel", "arbitrary")),
        name="rwkv_scan",
    )(*seqs, *pars)


def _swa_kernel(sink_ref, q_ref, kp_ref, kc_ref, vp_ref, vc_ref, o_ref, *, n_kv, q_per_kv, blocks_per_seq):
    first = pl.program_id(0) % blocks_per_seq == 0
    lane = lax.broadcasted_iota(jnp.int32, (ATT_BLOCK, LANES), 1)
    even = lane < HEAD
    qi = lax.broadcasted_iota(jnp.int32, (ATT_BLOCK, 2 * ATT_BLOCK), 0)
    kc = lax.broadcasted_iota(jnp.int32, (ATT_BLOCK, 2 * ATT_BLOCK), 1)
    valid = (kc > qi) & (kc <= qi + ATT_BLOCK) & ((kc >= ATT_BLOCK) | jnp.logical_not(first))
    bias = jnp.where(valid, 0.0, -jnp.inf)
    for gi in range(n_kv):
        cols = slice(gi * LANES, (gi + 1) * LANES)
        keys = jnp.concatenate([kp_ref[:, cols], kc_ref[:, cols]], axis=0)
        vals = jnp.concatenate([vp_ref[:, cols], vc_ref[:, cols]], axis=0)

        def head_pair(tile):
            q2 = q_ref[:, tile * LANES:(tile + 1) * LANES] * (HEAD ** -0.5)
            sinks = [sink_ref[2 * tile + half] for half in range(2)]
            logits = [lax.dot_general(jnp.where(even if half == 0 else jnp.logical_not(even), q2, jnp.zeros_like(q2)),
                                      keys, _NT, preferred_element_type=F32) for half in range(2)]
            yield
            logits = [lg + bias for lg in logits]
            mx = [jnp.maximum(jnp.max(lg, axis=-1, keepdims=True), s) for lg, s in zip(logits, sinks)]
            yield
            p = [jnp.exp(lg - m) for lg, m in zip(logits, mx)]
            denom = [jnp.sum(pp, axis=-1, keepdims=True) + jnp.exp(s - m) for pp, s, m in zip(p, sinks, mx)]
            yield
            outs = [_dot((pp * (1.0 / dn)).astype(BF16), vals) for pp, dn in zip(p, denom)]
            yield
            o_ref[:, tile * LANES:(tile + 1) * LANES] = jnp.where(even, outs[0], outs[1]).astype(o_ref.dtype)

        tiles = [head_pair(gi * (q_per_kv // 2) + pi) for pi in range(q_per_kv // 2)]
        done = object()
        while any([next(t, done) is not done for t in tiles]):
            pass


def _swa(qkv, sinks, seq_len, d, n_kv):
    m = qkv.shape[0]
    n_q = d // HEAD
    q_tiles = d // LANES
    blocks_per_seq = seq_len // ATT_BLOCK
    kw = n_kv * LANES
    k_blk = d // kw
    prev = lambda i: jnp.maximum(i - 1, 0)
    return pl.pallas_call(
        functools.partial(_swa_kernel, n_kv=n_kv, q_per_kv=n_q // n_kv, blocks_per_seq=blocks_per_seq),
        out_shape=jax.ShapeDtypeStruct((m, d), BF16),
        grid=(m // ATT_BLOCK,),
        in_specs=[
            pl.BlockSpec(memory_space=pltpu.SMEM),
            pl.BlockSpec((ATT_BLOCK, d), lambda i: (i, 0)),
            pl.BlockSpec((ATT_BLOCK, kw), lambda i: (prev(i), k_blk)),
            pl.BlockSpec((ATT_BLOCK, kw), lambda i: (i, k_blk)),
            pl.BlockSpec((ATT_BLOCK, kw), lambda i: (prev(i), k_blk + 1)),
            pl.BlockSpec((ATT_BLOCK, kw), lambda i: (i, k_blk + 1)),
        ],
        out_specs=pl.BlockSpec((ATT_BLOCK, d), lambda i: (i, 0)),
        compiler_params=_params(("parallel",)),
        name="swa",
    )(sinks, qkv, qkv, qkv, qkv, qkv)


def _conv_kernel(x_ref, p_ref, g_ref, wb_ref, wc_ref, wh_ref, cw_ref, o_ref, xn_ref, *, tiles_per_seq):
    @pl.when(pl.program_id(1) == 0)
    def _():
        g = g_ref[...]
        xn_ref[:SUBLANES, :] = _rms(p_ref[...], g).astype(BF16)
        xn_ref[SUBLANES:, :] = _rms(x_ref[...], g).astype(BF16)

    xn = xn_ref[...]
    u = _dot(xn, wc_ref[...]) * _dot(xn, wh_ref[...])
    row = lax.broadcasted_iota(jnp.int32, u.shape, 0)
    first = pl.program_id(0) % tiles_per_seq == 0
    u = jnp.where((row < SUBLANES) & first, 0.0, u)
    cw = cw_ref[...]
    uc = (cw[2:3, :] * u + cw[1:2, :] * pltpu.roll(u, 1, 0) + cw[0:1, :] * pltpu.roll(u, 2, 0))[SUBLANES:, :]
    bg = _dot(xn[SUBLANES:, :], wb_ref[...])
    o_ref[...] = (bg * uc).astype(o_ref.dtype)


def _conv_front(x, g, w_in, conv_w, layer, seq_len, *, tm=512, tn=1024):
    m, d = x.shape
    tm, tn = _tile(seq_len, tm), _tile(d, tn)
    nj = d // tn
    blk = tm // SUBLANES
    return pl.pallas_call(
        functools.partial(_conv_kernel, tiles_per_seq=seq_len // tm),
        out_shape=jax.ShapeDtypeStruct((m, d), BF16),
        grid=(m // tm, nj),
        in_specs=[
            pl.BlockSpec((tm, d), lambda i, j: (i, 0)),
            pl.BlockSpec((SUBLANES, d), lambda i, j: (jnp.maximum(i * blk - 1, 0), 0)),
            pl.BlockSpec((1, d), lambda i, j: (0, 0)),
            pl.BlockSpec((None, d, tn), lambda i, j: (layer, 0, j)),
            pl.BlockSpec((None, d, tn), lambda i, j: (layer, 0, nj + j)),
            pl.BlockSpec((None, d, tn), lambda i, j: (layer, 0, 2 * nj + j)),
            pl.BlockSpec((None, conv_w.shape[1], tn), lambda i, j: (layer, 0, j)),
        ],
        out_specs=pl.BlockSpec((tm, tn), lambda i, j: (i, j)),
        scratch_shapes=[pltpu.VMEM((SUBLANES + tm, d), BF16)],
        compiler_params=_params(("parallel", "arbitrary")),
        name="conv_front",
    )(x, x, g.reshape(1, d), w_in, w_in, w_in, conv_w)


def _rwkv_layer(x, seq_len, g_norm, mu, w_rkv, w_o, layer, w0, w1, w2, a0, a1, a2, g1, g2,
                k_k, k_a, r_k, ln_w, ln_b, v_first, vres):
    outs = _rwkv_mix(x, g_norm, mu, seq_len, w0, w1, w2, a0, a1, a2, g1, g2, vres)
    xr, xk, xv, w, a, g = outs[:6]
    r = _mm(xr, w_rkv, w_lead=(layer, 0), name="rwkv_r")
    k = _mm(xk, w_rkv, w_lead=(layer, 1), name="rwkv_k")
    v = _mm(xv, w_rkv, w_lead=(layer, 2), name="rwkv_v")
    if vres is None:
        scan_vres = None
        v_first = v
    else:
        scan_vres = (v_first, outs[6])
    yg = _rwkv_scan(r, k, v, w, a, g, k_k, k_a, r_k.reshape(-1), ln_w, ln_b, seq_len, scan_vres)
    return _mm(yg, w_o, w_lead=(layer,), res=x, name="rwkv_o"), v_first


def _swa_layer(x, seq_len, g_norm, w_qkv, b_qkv, w_o, layer, b_o, sinks):
    d = x.shape[1]
    n_kv = (w_qkv.shape[1] - d) // (2 * HEAD)
    kvd = n_kv * HEAD

    def dup(t):
        lead = t.shape[:-1]
        t = t.reshape(lead + (n_kv, 1, HEAD))
        return jnp.broadcast_to(t, lead + (n_kv, 2, HEAD)).reshape(lead + (2 * kvd,))

    w_cat = jnp.concatenate([w_qkv[:, :d], dup(w_qkv[:, d:d + kvd]), dup(w_qkv[:, d + kvd:])], axis=1)
    b_cat = jnp.concatenate([b_qkv[:d], dup(b_qkv[d:d + kvd]), dup(b_qkv[d + kvd:])])
    qkv = _mm(x, w_cat.astype(BF16), norm_g=g_norm, bias=b_cat, out_dtype=BF16, name="swa_qkv")
    o = _swa(qkv, sinks, seq_len, d, n_kv)
    return _mm(o, w_o, w_lead=(layer,), bias=b_o, res=x, name="swa_o")


def _conv_layer(x, seq_len, g_norm, w_in, conv_w, w_out, layer):
    gated = _conv_front(x, g_norm, w_in, conv_w, layer, seq_len)
    return _mm(gated, w_out, w_lead=(layer,), res=x, name="conv_o")


def kernel(x, norm_mix, norm_ffn, norm_final, rwkv_mu, rwkv_w_rkv, rwkv_w_o, rwkv_w0, rwkv_w1, rwkv_w2, rwkv_a0, rwkv_a1, rwkv_a2, rwkv_v0, rwkv_v1, rwkv_v2, rwkv_g1, rwkv_g2, rwkv_k_k, rwkv_k_a, rwkv_r_k, rwkv_lnx_w, rwkv_lnx_b, swa_w_qkv, swa_b_qkv, swa_w_o, swa_b_o, swa_sinks, conv_w_in, conv_w, conv_w_out, mlp_w_up, mlp_w_down):
    bsz, seq_len, d = x.shape
    depth = norm_mix.shape[0]
    xf = x.reshape(bsz * seq_len, d)
    rwkv_w_rkv, rwkv_w_o, swa_w_o, conv_w_in, conv_w_out, mlp_w_up, mlp_w_down = (
        t.astype(BF16) for t in (rwkv_w_rkv, rwkv_w_o, swa_w_o, conv_w_in, conv_w_out, mlp_w_up, mlp_w_down))
    v_first = None
    ia = ib = ic = 0
    for i in range(depth):
        kind = i % 3
        if kind == 0:
            vres = None if ia == 0 else (rwkv_v0[ia - 1], rwkv_v1[ia - 1], rwkv_v2[ia - 1])
            xf, v_first = _rwkv_layer(
                xf, seq_len, norm_mix[i], rwkv_mu[ia], rwkv_w_rkv, rwkv_w_o, ia,
                rwkv_w0[ia], rwkv_w1[ia], rwkv_w2[ia], rwkv_a0[ia], rwkv_a1[ia], rwkv_a2[ia],
                rwkv_g1[ia], rwkv_g2[ia], rwkv_k_k[ia], rwkv_k_a[ia], rwkv_r_k[ia],
                rwkv_lnx_w[ia], rwkv_lnx_b[ia], v_first, vres)
            ia += 1
        elif kind == 1:
            xf = _swa_layer(xf, seq_len, norm_mix[i], swa_w_qkv[ib], swa_b_qkv[ib],
                            swa_w_o, ib, swa_b_o[ib], swa_sinks[ib])
            ib += 1
        else:
            xf = _conv_layer(xf, seq_len, norm_mix[i], conv_w_in, conv_w, conv_w_out, ic)
            ic += 1
        xf = _ffn(xf, norm_ffn[i], mlp_w_up, mlp_w_down, i, norm_final if i == depth - 1 else None)
    return xf.reshape(bsz, seq_len, d)
```

```python
import functools

import jax
import jax.numpy as jnp
from jax import lax
from jax.experimental import pallas as pl
from jax.experimental.pallas import tpu as pltpu

F32 = jnp.float32
BF16 = jnp.bfloat16

HEAD = 64
LANES = 128
SUBLANES = 8
CHUNK = 64
ATT_BLOCK = 128
STAGGER = 6
RMS_EPS = 1e-6
GN_EPS = 64e-5
DECAY_SCALE = 0.6065306597126334
VMEM_LIMIT = 56 * 1024 * 1024

_NT = (((1,), (1,)), ((), ()))
_TN = (((0,), (0,)), ((), ()))


def _tile(n, pref):
    t = min(n, pref)
    while n % t:
        t //= 2
    return t


def _params(sem):
    return pltpu.CompilerParams(dimension_semantics=sem, vmem_limit_bytes=VMEM_LIMIT)


def _rms(x, g):
    ms = jnp.mean(x * x, axis=-1, keepdims=True)
    return x * lax.rsqrt(ms + RMS_EPS) * g


def _dot(a, b):
    return jnp.dot(a, b, preferred_element_type=F32)


def _split_hl(x):
    hi = x.astype(BF16)
    lo = (x - hi.astype(F32)).astype(BF16)
    return jnp.concatenate([hi, lo], axis=1)


def _mm_kernel(*refs, has_norm, has_bias, has_res):
    it = iter(refs)
    x_ref = next(it)
    g_ref = next(it) if has_norm else None
    w_ref = next(it)
    b_ref = next(it) if has_bias else None
    r_ref = next(it) if has_res else None
    o_ref = next(it)
    xn_ref = next(it) if has_norm else None

    if has_norm:
        @pl.when(pl.program_id(1) == 0)
        def _():
            xn_ref[...] = _rms(x_ref[...], g_ref[...]).astype(BF16)
        xb = xn_ref[...]
    else:
        xb = x_ref[...]
    acc = _dot(xb, w_ref[...])
    if has_bias:
        acc = acc + b_ref[...]
    if has_res:
        acc = acc + r_ref[...]
    o_ref[...] = acc.astype(o_ref.dtype)


def _mm(x, w, *, w_lead=(), norm_g=None, bias=None, res=None, out_dtype=F32, tm=1024, tn=4096, name="mm"):
    m, k = x.shape
    n = w.shape[-1]
    squeezed = (None,) * len(w_lead)
    tm, tn = _tile(m, tm), _tile(n, tn)
    has_norm, has_bias, has_res = norm_g is not None, bias is not None, res is not None
    resident = dict(pipeline_mode=pl.Buffered(1)) if tn == n else {}
    args, specs = [x], [pl.BlockSpec((tm, k), lambda i, j: (i, 0))]
    if has_norm:
        args.append(norm_g.reshape(1, k))
        specs.append(pl.BlockSpec((1, k), lambda i, j: (0, 0)))
    args.append(w)
    specs.append(pl.BlockSpec(squeezed + (k, tn), lambda i, j: w_lead + (0, j), **resident))
    if has_bias:
        args.append(bias.reshape(1, n))
        specs.append(pl.BlockSpec((1, tn), lambda i, j: (0, j)))
    if has_res:
        args.append(res)
        specs.append(pl.BlockSpec((tm, tn), lambda i, j: (i, j)))
    return pl.pallas_call(
        functools.partial(_mm_kernel, has_norm=has_norm, has_bias=has_bias, has_res=has_res),
        out_shape=jax.ShapeDtypeStruct((m, n), out_dtype),
        grid=(m // tm, n // tn),
        in_specs=specs,
        out_specs=pl.BlockSpec((tm, tn), lambda i, j: (i, j)),
        scratch_shapes=[pltpu.VMEM((tm, k), BF16)] if has_norm else [],
        compiler_params=_params(("parallel", "arbitrary")),
        name=name,
    )(*args)


def _ffn_kernel(*refs, has_final):
    if has_final:
        x_ref, g_ref, wu_ref, wd_ref, gf_ref, o_ref, xn_ref = refs
    else:
        x_ref, g_ref, wu_ref, wd_ref, o_ref, xn_ref = refs
    f = pl.program_id(1)

    @pl.when(f == 0)
    def _():
        x = x_ref[...]
        xn_ref[...] = _rms(x, g_ref[...]).astype(BF16)
        o_ref[...] = x

    h = jnp.maximum(_dot(xn_ref[...], wu_ref[...]), 0.0)
    o_ref[...] += _dot((h * h).astype(BF16), wd_ref[...])

    if has_final:
        @pl.when(f == pl.num_programs(1) - 1)
        def _():
            o_ref[...] = _rms(o_ref[...], gf_ref[...])


def _ffn(x, g, w_up, w_down, layer, final_g=None, *, tm=512, tf=1024):
    m, d = x.shape
    ff = w_up.shape[-1]
    tm, tf = _tile(m, tm), _tile(ff, tf)
    has_final = final_g is not None
    args = [x, g.reshape(1, d), w_up, w_down]
    specs = [
        pl.BlockSpec((tm, d), lambda i, f: (i, 0)),
        pl.BlockSpec((1, d), lambda i, f: (0, 0)),
        pl.BlockSpec((None, d, tf), lambda i, f: (layer, 0, f)),
        pl.BlockSpec((None, tf, d), lambda i, f: (layer, f, 0)),
    ]
    if has_final:
        args.append(final_g.reshape(1, d))
        specs.append(pl.BlockSpec((1, d), lambda i, f: (0, 0)))
    return pl.pallas_call(
        functools.partial(_ffn_kernel, has_final=has_final),
        out_shape=jax.ShapeDtypeStruct((m, d), F32),
        grid=(m // tm, ff // tf),
        in_specs=specs,
        out_specs=pl.BlockSpec((tm, d), lambda i, f: (i, 0)),
        scratch_shapes=[pltpu.VMEM((tm, d), BF16)],
        compiler_params=_params(("parallel", "arbitrary")),
        name="ffn",
    )(*args)


def _mix_kernel(*refs, tiles_per_seq, has_vres, ranks):
    it = iter(refs)
    x_ref, p_ref, g_ref, mu_ref, a_ref = (next(it) for _ in range(5))
    w2_ref, a2_ref, g2_ref = next(it), next(it), next(it)
    v2_ref = next(it) if has_vres else None
    w0_ref, a0_ref = next(it), next(it)
    v0_ref = next(it) if has_vres else None
    xr_ref, xk_ref, xv_ref, w_ref, ao_ref, go_ref = (next(it) for _ in range(6))
    vg_ref = next(it) if has_vres else None

    g = g_ref[...]
    h = _rms(x_ref[...], g)
    hp = _rms(p_ref[...], g)[SUBLANES - 1:SUBLANES, :]
    hp = jnp.where(pl.program_id(0) % tiles_per_seq == 0, 0.0, hp)
    row = lax.broadcasted_iota(jnp.int32, h.shape, 0)
    shifted = jnp.where(row == 0, hp, pltpu.roll(h, 1, 0))
    xx = shifted - h

    def mix(c):
        return (h + xx * mu_ref[c:c + 1, :]).astype(BF16)

    rw, ra, rg, rv = ranks

    def branch(c, lo, hi, mid_act, second_ref, bias_ref, out_act, out_ref, x_out_ref=None):
        xm = mix(c)
        if x_out_ref is not None:
            x_out_ref[...] = xm
        mid = _dot(xm, a_ref[:, lo:hi])
        yield
        out = _dot(mid_act(mid).astype(BF16), second_ref[...])
        yield
        if bias_ref is not None:
            out = out + bias_ref[...]
        out_ref[...] = out_act(out).astype(out_ref.dtype)

    def plain(c, x_out_ref):
        x_out_ref[...] = mix(c)
        yield

    ident = lambda t: t
    branches = [
        branch(1, 0, rw, jnp.tanh, w2_ref, w0_ref, ident, w_ref),
        plain(0, xr_ref),
        branch(4, rw, rw + ra, ident, a2_ref, a0_ref, jax.nn.sigmoid, ao_ref),
        plain(2, xk_ref),
        branch(5, rw + ra, rw + ra + rg, jax.nn.sigmoid, g2_ref, None, ident, go_ref),
    ]
    if has_vres:
        branches.append(branch(3, rw + ra + rg, rw + ra + rg + rv, ident, v2_ref, v0_ref, jax.nn.sigmoid, vg_ref, xv_ref))
    else:
        branches.append(plain(3, xv_ref))
    done = object()
    while any([next(b, done) is not done for b in branches]):
        pass


def _pad_rank(a, b):
    pad = -a.shape[1] % LANES
    return jnp.pad(a, ((0, 0), (0, pad))).astype(BF16), jnp.pad(b, ((0, pad), (0, 0))).astype(BF16)


def _rwkv_mix(x, g, mu, seq_len, w0, w1, w2, a0, a1, a2, g1, g2, vres, *, tm=256):
    m, d = x.shape
    tm = _tile(seq_len, tm)
    blk = tm // SUBLANES
    has_vres = vres is not None
    pairs = [_pad_rank(w1, w2), _pad_rank(a1, a2), _pad_rank(g1, g2)]
    biases = [w0, a0]
    if has_vres:
        pairs.append(_pad_rank(vres[1], vres[2]))
        biases.append(vres[0])
    first = jnp.concatenate([p[0] for p in pairs], axis=1)
    ranks = tuple(p[0].shape[1] for p in pairs) + ((0,) if not has_vres else ())
    row_spec = pl.BlockSpec((tm, d), lambda i: (i, 0))
    full = lambda arr: pl.BlockSpec(arr.shape, lambda i: (0, 0))
    seconds = [p[1] for p in pairs]
    biases = [b.reshape(1, d) for b in biases]
    out_dtypes = [BF16, BF16, BF16, F32, F32, BF16] + ([F32] if has_vres else [])
    outs = pl.pallas_call(
        functools.partial(_mix_kernel, tiles_per_seq=seq_len // tm, has_vres=has_vres, ranks=ranks),
        out_shape=[jax.ShapeDtypeStruct((m, d), dt) for dt in out_dtypes],
        grid=(m // tm,),
        in_specs=[
            row_spec,
            pl.BlockSpec((SUBLANES, d), lambda i: (jnp.maximum(i * blk - 1, 0), 0)),
            pl.BlockSpec((1, d), lambda i: (0, 0)),
            full(mu), full(first),
        ] + [full(s) for s in seconds] + [full(b) for b in biases],
        out_specs=[row_spec] * len(out_dtypes),
        compiler_params=_params(("parallel",)),
        name="rwkv_mix",
    )(x, x, g.reshape(1, d), mu, first, *seconds, *biases)
    return outs


def _scan_kernel(*refs, has_vres, n_chunks, n_pairs):
    if has_vres:
        (r_ref, k_ref, v_ref, w_ref, a_ref, g_ref, vf_ref, vg_ref,
         kk_ref, ka_ref, rk_ref, lw_ref, lb_ref, o_ref, ht_ref) = refs
    else:
        (r_ref, k_ref, v_ref, w_ref, a_ref, g_ref,
         kk_ref, ka_ref, rk_ref, lw_ref, lb_ref, o_ref, ht_ref) = refs
    c2 = 2 * CHUNK

    @pl.when(pl.program_id(2) == 0)
    def _():
        ht_ref[...] = jnp.zeros_like(ht_ref)

    lane = lax.broadcasted_iota(jnp.int32, (CHUNK, LANES), 1)
    head0 = lane < HEAD

    def both(x, y):
        return jnp.concatenate([x, y], axis=0)

    def split(x0, x1):
        return jnp.where(head0, x0, x1)

    ri = lax.broadcasted_iota(jnp.int32, (LANES, LANES), 0)
    ci = lax.broadcasted_iota(jnp.int32, (LANES, LANES), 1)

    def head_sum(x):
        s0 = jnp.sum(jnp.where(head0, x, 0.0), axis=-1, keepdims=True)
        s1 = jnp.sum(jnp.where(head0, 0.0, x), axis=-1, keepdims=True)
        return jnp.where(head0, s0, s1)

    eye = (ri == ci).astype(F32)
    same_head = (ri // HEAD) == (ci // HEAD)
    ti = lax.broadcasted_iota(jnp.int32, (CHUNK, CHUNK), 0)
    si = lax.broadcasted_iota(jnp.int32, (CHUNK, CHUNK), 1)
    tri = (ti >= si).astype(BF16)
    ts = lax.broadcasted_iota(jnp.int32, (CHUNK, LANES), 0)
    strict = ts > lane % HEAD
    incl = ts >= lane % HEAD

    ops, ys = {}, {}

    def head(p, c):
        rows = slice(c * CHUNK, (c + 1) * CHUNK)
        cols = slice(p * LANES, (p + 1) * LANES)
        r, k, v, a = r_ref[rows, cols], k_ref[rows, cols], v_ref[rows, cols], a_ref[rows, cols]
        k_k, k_a, r_k = kk_ref[:, cols], ka_ref[:, cols], rk_ref[:, cols]
        if has_vres:
            v = v + (vf_ref[rows, cols] - v) * vg_ref[rows, cols]
        lw = -DECAY_SCALE * jax.nn.sigmoid(w_ref[rows, cols])
        lp = _dot(tri, _split_hl(lw))
        kk = k * k_k
        norm = jnp.sqrt(head_sum(kk * kk))
        yield
        kmod = k * (1.0 + (a - 1.0) * k_a)
        bonus = head_sum(r * kmod * r_k)
        yield
        lp = lp[:, :LANES] + lp[:, LANES:]
        lp_end = lp[CHUNK - 1:CHUNK, :]
        en = jnp.exp(-lp)
        to_end = jnp.exp(lp_end - lp)
        r_t = r * jnp.exp(lp)
        yield
        kk = kk / jnp.maximum(norm, 1e-12)
        bb = kk * a
        a_t = -kk * jnp.exp(lp - lw)
        b_t = (bb * en).astype(BF16)
        k_t = (kmod * en).astype(BF16)
        v_b = v.astype(BF16)
        yield
        ops[p, c] = dict(
            lhs0=both(jnp.where(head0, a_t, 0.0), jnp.where(head0, r_t, 0.0)).astype(BF16),
            lhs1=both(jnp.where(head0, 0.0, a_t), jnp.where(head0, 0.0, r_t)).astype(BF16),
            rhs0=both(b_t, k_t), rhs1=both(k_t, b_t),
            ar=both(a_t, r_t).astype(BF16), v_b=v_b,
            bk_end=both((bb * to_end).astype(BF16), (kmod * to_end).astype(BF16)),
            decay_end=jnp.exp(lp_end), bonus_v=bonus * v)
        yield

    def mid(p, c):
        o = ops.pop((p, c))
        v_b = o["v_b"]
        g0 = lax.dot_general(o["lhs0"], o["rhs0"], _NT, preferred_element_type=F32)
        g1 = lax.dot_general(o["lhs1"], o["rhs1"], _NT, preferred_element_type=F32)
        yield
        top0 = jnp.where(strict, g0[:CHUNK, :], 0.0)
        top1 = jnp.where(strict, g1[:CHUNK, :], 0.0)
        bot0 = jnp.where(incl, g0[CHUNK:, :], 0.0).astype(BF16)
        bot1 = jnp.where(incl, g1[CHUNK:, :], 0.0).astype(BF16)
        n_ab = both(jnp.where(head0, top0, 0.0), jnp.where(head0, 0.0, top1))
        n_ak = both(jnp.where(head0, 0.0, top0), jnp.where(head0, top1, 0.0)).astype(BF16)

        inv = eye + n_ab
        pw = n_ab.astype(BF16)
        pw = _dot(pw, pw).astype(BF16)
        yield
        for _ in range(CHUNK.bit_length() - 3):
            prod = _dot(pw, jnp.concatenate([pw, inv.astype(BF16)], axis=1))
            yield
            pw = prod[:, :c2].astype(BF16)
            inv = inv + prod[:, c2:]
        inv = inv + _dot(pw, inv.astype(BF16))

        ht = ht_ref[p]
        state_proj = lax.dot_general(o["ar"], ht.astype(BF16), _NT, preferred_element_type=F32)
        xv = _dot(n_ak, both(v_b, v_b))
        yield
        sp_a = state_proj[:CHUNK, :]
        xs = both(jnp.where(head0, xv[:CHUNK, :] + sp_a, 0.0), jnp.where(head0, 0.0, xv[CHUNK:, :] + sp_a))
        us = _dot(inv.astype(BF16), xs.astype(BF16))
        yield
        u_b = (us[:CHUNK, :] + us[CHUNK:, :]).astype(BF16)
        y = state_proj[CHUNK:, :] + split(_dot(bot0, both(u_b, v_b)), _dot(bot1, both(v_b, u_b)))
        new = lax.dot_general(both(u_b, v_b), o["bk_end"], _TN, preferred_element_type=F32)
        ht_ref[p] = ht * o["decay_end"] + jnp.where(same_head, new, 0.0)
        yield
        ys[p, c] = (y, o["bonus_v"])
        yield

    def tail(p, c):
        rows = slice(c * CHUNK, (c + 1) * CHUNK)
        cols = slice(p * LANES, (p + 1) * LANES)
        y, bonus_v = ys.pop((p, c))
        mean = head_sum(y) * (1.0 / HEAD)
        yield
        yc = y - mean
        var = head_sum(yc * yc) * (1.0 / HEAD)
        yield
        out = yc * lax.rsqrt(var + GN_EPS) * lw_ref[:, cols] + lb_ref[:, cols] + bonus_v
        o_ref[rows, cols] = (out * g_ref[rows, cols]).astype(o_ref.dtype)
        yield

    done = object()

    def pipeline(p):
        yield from head(p, 0)
        for c in range(n_chunks):
            main = mid(p, c)
            sides = ([head(p, c + 1)] if c + 1 < n_chunks else []) + ([tail(p, c - 1)] if c > 0 else [])
            step, busy = 0, True
            while busy:
                busy = next(main, done) is not done
                if sides:
                    side = sides[step % len(sides)]
                    if next(side, done) is done:
                        sides.remove(side)
                    else:
                        busy = True
                step += 1
                yield
        yield from tail(p, n_chunks - 1)

    programs = [pipeline(p) for p in range(n_pairs)]
    lag = [0 if p < n_pairs // 2 else STAGGER for p in range(n_pairs)]
    step, live = 0, n_pairs
    while live:
        live = 0
        for p, prog in enumerate(programs):
            if step >= lag[p] and next(prog, done) is not done:
                live += 1
            elif step < lag[p]:
                live += 1
        step += 1


def _rwkv_scan(r, k, v, w, a, g, k_k, k_a, r_k, ln_w, ln_b, seq_len, vres=None, *, tc=256, n_pairs=16):
    m, d = r.shape
    tc = _tile(seq_len, tc)
    nt = seq_len // tc
    n_pairs = _tile(d // LANES, n_pairs)
    width = n_pairs * LANES
    row_spec = pl.BlockSpec((tc, width), lambda b, h, c: (b * nt + c, h))
    par_spec = pl.BlockSpec((1, width), lambda b, h, c: (0, h))
    seqs = [r, k, v, w, a, g] + (list(vres) if vres is not None else [])
    pars = [p.reshape(1, d) for p in (k_k, k_a, r_k, ln_w, ln_b)]
    return pl.pallas_call(
        functools.partial(_scan_kernel, has_vres=vres is not None, n_chunks=tc // CHUNK, n_pairs=n_pairs),
        out_shape=jax.ShapeDtypeStruct((m, d), BF16),
        grid=(m // seq_len, d // width, nt),
        in_specs=[row_spec] * len(seqs) + [par_spec] * len(pars),
        out_specs=row_spec,
        scratch_shapes=[pltpu.VMEM((n_pairs, LANES, LANES), F32)],
        compiler_params=_params(("parallel", "parallel", "arbitrary")),
        name="rwkv_scan",
    )(*seqs, *pars)


def _swa_kernel(sink_ref, q_ref, kp_ref, kc_ref, vp_ref, vc_ref, o_ref, *, n_kv, q_per_kv, blocks_per_seq):
    first = pl.program_id(0) % blocks_per_seq == 0
    lane = lax.broadcasted_iota(jnp.int32, (ATT_BLOCK, LANES), 1)
    even = lane < HEAD
    qi = lax.broadcasted_iota(jnp.int32, (ATT_BLOCK, 2 * ATT_BLOCK), 0)
    kc = lax.broadcasted_iota(jnp.int32, (ATT_BLOCK, 2 * ATT_BLOCK), 1)
    valid = (kc > qi) & (kc <= qi + ATT_BLOCK) & ((kc >= ATT_BLOCK) | jnp.logical_not(first))
    bias = jnp.where(valid, 0.0, -jnp.inf)
    for gi in range(n_kv):
        cols = slice(gi * LANES, (gi + 1) * LANES)
        keys = jnp.concatenate([kp_ref[:, cols], kc_ref[:, cols]], axis=0)
        vals = jnp.concatenate([vp_ref[:, cols], vc_ref[:, cols]], axis=0)

        def head_pair(tile):
            q2 = q_ref[:, tile * LANES:(tile + 1) * LANES] * (HEAD ** -0.5)
            sinks = [sink_ref[2 * tile + half] for half in range(2)]
            logits = [lax.dot_general(jnp.where(even if half == 0 else jnp.logical_not(even), q2, jnp.zeros_like(q2)),
                                      keys, _NT, preferred_element_type=F32) for half in range(2)]
            yield
            logits = [lg + bias for lg in logits]
            mx = [jnp.maximum(jnp.max(lg, axis=-1, keepdims=True), s) for lg, s in zip(logits, sinks)]
            yield
            p = [jnp.exp(lg - m) for lg, m in zip(logits, mx)]
            denom = [jnp.sum(pp, axis=-1, keepdims=True) + jnp.exp(s - m) for pp, s, m in zip(p, sinks, mx)]
            yield
            outs = [_dot((pp * (1.0 / dn)).astype(BF16), vals) for pp, dn in zip(p, denom)]
            yield
            o_ref[:, tile * LANES:(tile + 1) * LANES] = jnp.where(even, outs[0], outs[1]).astype(o_ref.dtype)

        tiles = [head_pair(gi * (q_per_kv // 2) + pi) for pi in range(q_per_kv // 2)]
        done = object()
        while any([next(t, done) is not done for t in tiles]):
            pass


def _swa(qkv, sinks, seq_len, d, n_kv):
    m = qkv.shape[0]
    n_q = d // HEAD
    q_tiles = d // LANES
    blocks_per_seq = seq_len // ATT_BLOCK
    kw = n_kv * LANES
    k_blk = d // kw
    prev = lambda i: jnp.maximum(i - 1, 0)
    return pl.pallas_call(
        functools.partial(_swa_kernel, n_kv=n_kv, q_per_kv=n_q // n_kv, blocks_per_seq=blocks_per_seq),
        out_shape=jax.ShapeDtypeStruct((m, d), BF16),
        grid=(m // ATT_BLOCK,),
        in_specs=[
            pl.BlockSpec(memory_space=pltpu.SMEM),
            pl.BlockSpec((ATT_BLOCK, d), lambda i: (i, 0)),
            pl.BlockSpec((ATT_BLOCK, kw), lambda i: (prev(i), k_blk)),
            pl.BlockSpec((ATT_BLOCK, kw), lambda i: (i, k_blk)),
            pl.BlockSpec((ATT_BLOCK, kw), lambda i: (prev(i), k_blk + 1)),
            pl.BlockSpec((ATT_BLOCK, kw), lambda i: (i, k_blk + 1)),
        ],
        out_specs=pl.BlockSpec((ATT_BLOCK, d), lambda i: (i, 0)),
        compiler_params=_params(("parallel",)),
        name="swa",
    )(sinks, qkv, qkv, qkv, qkv, qkv)


def _conv_kernel(x_ref, p_ref, g_ref, wb_ref, wc_ref, wh_ref, cw_ref, o_ref, xn_ref, *, tiles_per_seq):
    @pl.when(pl.program_id(1) == 0)
    def _():
        g = g_ref[...]
        xn_ref[:SUBLANES, :] = _rms(p_ref[...], g).astype(BF16)
        xn_ref[SUBLANES:, :] = _rms(x_ref[...], g).astype(BF16)

    xn = xn_ref[...]
    u = _dot(xn, wc_ref[...]) * _dot(xn, wh_ref[...])
    row = lax.broadcasted_iota(jnp.int32, u.shape, 0)
    first = pl.program_id(0) % tiles_per_seq == 0
    u = jnp.where((row < SUBLANES) & first, 0.0, u)
    cw = cw_ref[...]
    uc = (cw[2:3, :] * u + cw[1:2, :] * pltpu.roll(u, 1, 0) + cw[0:1, :] * pltpu.roll(u, 2, 0))[SUBLANES:, :]
    bg = _dot(xn[SUBLANES:, :], wb_ref[...])
    o_ref[...] = (bg * uc).astype(o_ref.dtype)


def _conv_front(x, g, w_in, conv_w, layer, seq_len, *, tm=512, tn=2048):
    m, d = x.shape
    tm, tn = _tile(seq_len, tm), _tile(d, tn)
    nj = d // tn
    blk = tm // SUBLANES
    resident = dict(pipeline_mode=pl.Buffered(1)) if nj == 1 else {}
    return pl.pallas_call(
        functools.partial(_conv_kernel, tiles_per_seq=seq_len // tm),
        out_shape=jax.ShapeDtypeStruct((m, d), BF16),
        grid=(m // tm, nj),
        in_specs=[
            pl.BlockSpec((tm, d), lambda i, j: (i, 0)),
            pl.BlockSpec((SUBLANES, d), lambda i, j: (jnp.maximum(i * blk - 1, 0), 0)),
            pl.BlockSpec((1, d), lambda i, j: (0, 0)),
            pl.BlockSpec((None, d, tn), lambda i, j: (layer, 0, j), **resident),
            pl.BlockSpec((None, d, tn), lambda i, j: (layer, 0, nj + j), **resident),
            pl.BlockSpec((None, d, tn), lambda i, j: (layer, 0, 2 * nj + j), **resident),
            pl.BlockSpec((None, conv_w.shape[1], tn), lambda i, j: (layer, 0, j)),
        ],
        out_specs=pl.BlockSpec((tm, tn), lambda i, j: (i, j)),
        scratch_shapes=[pltpu.VMEM((SUBLANES + tm, d), BF16)],
        compiler_params=_params(("parallel", "arbitrary")),
        name="conv_front",
    )(x, x, g.reshape(1, d), w_in, w_in, w_in, conv_w)


def _rwkv_layer(x, seq_len, g_norm, mu, w_rkv, w_o, layer, w0, w1, w2, a0, a1, a2, g1, g2,
                k_k, k_a, r_k, ln_w, ln_b, v_first, vres):
    outs = _rwkv_mix(x, g_norm, mu, seq_len, w0, w1, w2, a0, a1, a2, g1, g2, vres)
    xr, xk, xv, w, a, g = outs[:6]
    r = _mm(xr, w_rkv, w_lead=(layer, 0), name="rwkv_r")
    k = _mm(xk, w_rkv, w_lead=(layer, 1), name="rwkv_k")
    v = _mm(xv, w_rkv, w_lead=(layer, 2), name="rwkv_v")
    if vres is None:
        scan_vres = None
        v_first = v
    else:
        scan_vres = (v_first, outs[6])
    yg = _rwkv_scan(r, k, v, w, a, g, k_k, k_a, r_k.reshape(-1), ln_w, ln_b, seq_len, scan_vres)
    return _mm(yg, w_o, w_lead=(layer,), res=x, name="rwkv_o"), v_first


def _swa_layer(x, seq_len, g_norm, w_qkv, b_qkv, w_o, layer, b_o, sinks):
    d = x.shape[1]
    n_kv = (w_qkv.shape[1] - d) // (2 * HEAD)
    kvd = n_kv * HEAD

    def dup(t):
        lead = t.shape[:-1]
        t = t.reshape(lead + (n_kv, 1, HEAD))
        return jnp.broadcast_to(t, lead + (n_kv, 2, HEAD)).reshape(lead + (2 * kvd,))

    w_cat = jnp.concatenate([w_qkv[:, :d], dup(w_qkv[:, d:d + kvd]), dup(w_qkv[:, d + kvd:])], axis=1)
    b_cat = jnp.concatenate([b_qkv[:d], dup(b_qkv[d:d + kvd]), dup(b_qkv[d + kvd:])])
    qkv = _mm(x, w_cat.astype(BF16), norm_g=g_norm, bias=b_cat, out_dtype=BF16, name="swa_qkv")
    o = _swa(qkv, sinks, seq_len, d, n_kv)
    return _mm(o, w_o, w_lead=(layer,), bias=b_o, res=x, name="swa_o")


def _conv_layer(x, seq_len, g_norm, w_in, conv_w, w_out, layer):
    gated = _conv_front(x, g_norm, w_in, conv_w, layer, seq_len)
    return _mm(gated, w_out, w_lead=(layer,), res=x, name="conv_o")


def kernel(x, norm_mix, norm_ffn, norm_final, rwkv_mu, rwkv_w_rkv, rwkv_w_o, rwkv_w0, rwkv_w1, rwkv_w2, rwkv_a0, rwkv_a1, rwkv_a2, rwkv_v0, rwkv_v1, rwkv_v2, rwkv_g1, rwkv_g2, rwkv_k_k, rwkv_k_a, rwkv_r_k, rwkv_lnx_w, rwkv_lnx_b, swa_w_qkv, swa_b_qkv, swa_w_o, swa_b_o, swa_sinks, conv_w_in, conv_w, conv_w_out, mlp_w_up, mlp_w_down):
    bsz, seq_len, d = x.shape
    depth = norm_mix.shape[0]
    xf = x.reshape(bsz * seq_len, d)
    rwkv_w_rkv, rwkv_w_o, swa_w_o, conv_w_in, conv_w_out, mlp_w_up, mlp_w_down = (
        t.astype(BF16) for t in (rwkv_w_rkv, rwkv_w_o, swa_w_o, conv_w_in, conv_w_out, mlp_w_up, mlp_w_down))
    v_first = None
    ia = ib = ic = 0
    for i in range(depth):
        kind = i % 3
        if kind == 0:
            vres = None if ia == 0 else (rwkv_v0[ia - 1], rwkv_v1[ia - 1], rwkv_v2[ia - 1])
            xf, v_first = _rwkv_layer(
                xf, seq_len, norm_mix[i], rwkv_mu[ia], rwkv_w_rkv, rwkv_w_o, ia,
                rwkv_w0[ia], rwkv_w1[ia], rwkv_w2[ia], rwkv_a0[ia], rwkv_a1[ia], rwkv_a2[ia],
                rwkv_g1[ia], rwkv_g2[ia], rwkv_k_k[ia], rwkv_k_a[ia], rwkv_r_k[ia],
                rwkv_lnx_w[ia], rwkv_lnx_b[ia], v_first, vres)
            ia += 1
        elif kind == 1:
            xf = _swa_layer(xf, seq_len, norm_mix[i], swa_w_qkv[ib], swa_b_qkv[ib],
                            swa_w_o, ib, swa_b_o[ib], swa_sinks[ib])
            ib += 1
        else:
            xf = _conv_layer(xf, seq_len, norm_mix[i], conv_w_in, conv_w, conv_w_out, ic)
            ic += 1
        xf = _ffn(xf, norm_ffn[i], mlp_w_up, mlp_w_down, i, norm_final if i == depth - 1 else None)
    return xf.reshape(bsz, seq_len, d)
```

```python
import functools

import jax
import jax.numpy as jnp
from jax import lax
from jax.experimental import pallas as pl
from jax.experimental.pallas import tpu as pltpu

F32 = jnp.float32
BF16 = jnp.bfloat16

HEAD = 64
LANES = 128
SUBLANES = 8
CHUNK = 64
ATT_BLOCK = 128
STAGGER = 6
RMS_EPS = 1e-6
GN_EPS = 64e-5
DECAY_SCALE = 0.6065306597126334
VMEM_LIMIT = 56 * 1024 * 1024

_NT = (((1,), (1,)), ((), ()))
_TN = (((0,), (0,)), ((), ()))


def _tile(n, pref):
    t = min(n, pref)
    while n % t:
        t //= 2
    return t


def _params(sem):
    return pltpu.CompilerParams(dimension_semantics=sem, vmem_limit_bytes=VMEM_LIMIT)


def _rms(x, g):
    ms = jnp.mean(x * x, axis=-1, keepdims=True)
    return x * lax.rsqrt(ms + RMS_EPS) * g


def _dot(a, b):
    return jnp.dot(a, b, preferred_element_type=F32)


def _split_hl(x):
    hi = x.astype(BF16)
    lo = (x - hi.astype(F32)).astype(BF16)
    return jnp.concatenate([hi, lo], axis=1)


def _mm_kernel(*refs, has_norm, has_bias, has_res):
    it = iter(refs)
    x_ref = next(it)
    g_ref = next(it) if has_norm else None
    w_ref = next(it)
    b_ref = next(it) if has_bias else None
    r_ref = next(it) if has_res else None
    o_ref = next(it)
    xn_ref = next(it) if has_norm else None

    if has_norm:
        @pl.when(pl.program_id(1) == 0)
        def _():
            xn_ref[...] = _rms(x_ref[...], g_ref[...]).astype(BF16)
        xb = xn_ref[...]
    else:
        xb = x_ref[...]
    acc = _dot(xb, w_ref[...])
    if has_bias:
        acc = acc + b_ref[...]
    if has_res:
        acc = acc + r_ref[...]
    o_ref[...] = acc.astype(o_ref.dtype)


def _mm(x, w, *, w_lead=(), norm_g=None, bias=None, res=None, out_dtype=F32, tm=1024, tn=4096, name="mm"):
    m, k = x.shape
    n = w.shape[-1]
    squeezed = (None,) * len(w_lead)
    tm, tn = _tile(m, tm), _tile(n, tn)
    has_norm, has_bias, has_res = norm_g is not None, bias is not None, res is not None
    resident = dict(pipeline_mode=pl.Buffered(1)) if tn == n else {}
    args, specs = [x], [pl.BlockSpec((tm, k), lambda i, j: (i, 0))]
    if has_norm:
        args.append(norm_g.reshape(1, k))
        specs.append(pl.BlockSpec((1, k), lambda i, j: (0, 0)))
    args.append(w)
    specs.append(pl.BlockSpec(squeezed + (k, tn), lambda i, j: w_lead + (0, j), **resident))
    if has_bias:
        args.append(bias.reshape(1, n))
        specs.append(pl.BlockSpec((1, tn), lambda i, j: (0, j)))
    if has_res:
        args.append(res)
        specs.append(pl.BlockSpec((tm, tn), lambda i, j: (i, j)))
    return pl.pallas_call(
        functools.partial(_mm_kernel, has_norm=has_norm, has_bias=has_bias, has_res=has_res),
        out_shape=jax.ShapeDtypeStruct((m, n), out_dtype),
        grid=(m // tm, n // tn),
        in_specs=specs,
        out_specs=pl.BlockSpec((tm, tn), lambda i, j: (i, j)),
        scratch_shapes=[pltpu.VMEM((tm, k), BF16)] if has_norm else [],
        compiler_params=_params(("parallel", "arbitrary")),
        name=name,
    )(*args)


def _ffn_kernel(*refs, has_final, slice_rows):
    if has_final:
        x_ref, xnext_ref, g_ref, wu_ref, wd_ref, gf_ref, o_ref, xa_ref, xb_ref = refs
    else:
        x_ref, xnext_ref, g_ref, wu_ref, wd_ref, o_ref, xa_ref, xb_ref = refs
    i, f, nf = pl.program_id(0), pl.program_id(1), pl.num_programs(1)
    g = g_ref[...]

    @pl.when((i == 0) & (f == 0))
    def _():
        xa_ref[...] = _rms(x_ref[...], g).astype(BF16)

    @pl.when(f == 0)
    def _():
        o_ref[...] = x_ref[...]

    def body(cur_ref, nxt_ref):
        rows = pl.ds(pl.multiple_of(f * slice_rows, slice_rows), slice_rows)
        nxt_ref[rows, :] = _rms(xnext_ref[rows, :], g).astype(BF16)
        h = jnp.maximum(_dot(cur_ref[...], wu_ref[...]), 0.0)
        o_ref[...] += _dot((h * h).astype(BF16), wd_ref[...])

    @pl.when(i % 2 == 0)
    def _():
        body(xa_ref, xb_ref)

    @pl.when(i % 2 == 1)
    def _():
        body(xb_ref, xa_ref)

    if has_final:
        @pl.when(f == nf - 1)
        def _():
            o_ref[...] = _rms(o_ref[...], gf_ref[...])


def _ffn(x, g, w_up, w_down, layer, final_g=None, *, tm=512, tf=1024):
    m, d = x.shape
    ff = w_up.shape[-1]
    tm, tf = _tile(m, tm), _tile(ff, tf)
    slice_rows = tm // (ff // tf)
    assert slice_rows * (ff // tf) == tm and slice_rows % (2 * SUBLANES) == 0
    has_final = final_g is not None
    last = m // tm - 1
    args = [x, x, g.reshape(1, d), w_up, w_down]
    specs = [
        pl.BlockSpec((tm, d), lambda i, f: (i, 0)),
        pl.BlockSpec((tm, d), lambda i, f: (jnp.minimum(i + 1, last), 0)),
        pl.BlockSpec((1, d), lambda i, f: (0, 0)),
        pl.BlockSpec((None, d, tf), lambda i, f: (layer, 0, f)),
        pl.BlockSpec((None, tf, d), lambda i, f: (layer, f, 0)),
    ]
    if has_final:
        args.append(final_g.reshape(1, d))
        specs.append(pl.BlockSpec((1, d), lambda i, f: (0, 0)))
    return pl.pallas_call(
        functools.partial(_ffn_kernel, has_final=has_final, slice_rows=slice_rows),
        out_shape=jax.ShapeDtypeStruct((m, d), F32),
        grid=(m // tm, ff // tf),
        in_specs=specs,
        out_specs=pl.BlockSpec((tm, d), lambda i, f: (i, 0)),
        scratch_shapes=[pltpu.VMEM((tm, d), BF16), pltpu.VMEM((tm, d), BF16)],
        compiler_params=_params(("arbitrary", "arbitrary")),
        name="ffn",
    )(*args)


def _mix_kernel(*refs, tiles_per_seq, has_vres, ranks):
    it = iter(refs)
    x_ref, p_ref, g_ref, mu_ref, a_ref = (next(it) for _ in range(5))
    w2_ref, a2_ref, g2_ref = next(it), next(it), next(it)
    v2_ref = next(it) if has_vres else None
    w0_ref, a0_ref = next(it), next(it)
    v0_ref = next(it) if has_vres else None
    xr_ref, xk_ref, xv_ref, w_ref, ao_ref, go_ref = (next(it) for _ in range(6))
    vg_ref = next(it) if has_vres else None

    g = g_ref[...]
    h = _rms(x_ref[...], g)
    hp = _rms(p_ref[...], g)[SUBLANES - 1:SUBLANES, :]
    hp = jnp.where(pl.program_id(0) % tiles_per_seq == 0, 0.0, hp)
    row = lax.broadcasted_iota(jnp.int32, h.shape, 0)
    shifted = jnp.where(row == 0, hp, pltpu.roll(h, 1, 0))
    xx = shifted - h

    def mix(c):
        return (h + xx * mu_ref[c:c + 1, :]).astype(BF16)

    rw, ra, rg, rv = ranks

    def branch(c, lo, hi, mid_act, second_ref, bias_ref, out_act, out_ref, x_out_ref=None):
        xm = mix(c)
        if x_out_ref is not None:
            x_out_ref[...] = xm
        mid = _dot(xm, a_ref[:, lo:hi])
        yield
        out = _dot(mid_act(mid).astype(BF16), second_ref[...])
        yield
        if bias_ref is not None:
            out = out + bias_ref[...]
        out_ref[...] = out_act(out).astype(out_ref.dtype)

    def plain(c, x_out_ref):
        x_out_ref[...] = mix(c)
        yield

    ident = lambda t: t
    branches = [
        branch(1, 0, rw, jnp.tanh, w2_ref, w0_ref, ident, w_ref),
        plain(0, xr_ref),
        branch(4, rw, rw + ra, ident, a2_ref, a0_ref, jax.nn.sigmoid, ao_ref),
        plain(2, xk_ref),
        branch(5, rw + ra, rw + ra + rg, jax.nn.sigmoid, g2_ref, None, ident, go_ref),
    ]
    if has_vres:
        branches.append(branch(3, rw + ra + rg, rw + ra + rg + rv, ident, v2_ref, v0_ref, jax.nn.sigmoid, vg_ref, xv_ref))
    else:
        branches.append(plain(3, xv_ref))
    done = object()
    while any([next(b, done) is not done for b in branches]):
        pass


def _pad_rank(a, b):
    pad = -a.shape[1] % LANES
    return jnp.pad(a, ((0, 0), (0, pad))).astype(BF16), jnp.pad(b, ((0, pad), (0, 0))).astype(BF16)


def _rwkv_mix(x, g, mu, seq_len, w0, w1, w2, a0, a1, a2, g1, g2, vres, *, tm=256):
    m, d = x.shape
    tm = _tile(seq_len, tm)
    blk = tm // SUBLANES
    has_vres = vres is not None
    pairs = [_pad_rank(w1, w2), _pad_rank(a1, a2), _pad_rank(g1, g2)]
    biases = [w0, a0]
    if has_vres:
        pairs.append(_pad_rank(vres[1], vres[2]))
        biases.append(vres[0])
    first = jnp.concatenate([p[0] for p in pairs], axis=1)
    ranks = tuple(p[0].shape[1] for p in pairs) + ((0,) if not has_vres else ())
    row_spec = pl.BlockSpec((tm, d), lambda i: (i, 0))
    full = lambda arr: pl.BlockSpec(arr.shape, lambda i: (0, 0))
    seconds = [p[1] for p in pairs]
    biases = [b.reshape(1, d) for b in biases]
    out_dtypes = [BF16, BF16, BF16, F32, F32, BF16] + ([F32] if has_vres else [])
    outs = pl.pallas_call(
        functools.partial(_mix_kernel, tiles_per_seq=seq_len // tm, has_vres=has_vres, ranks=ranks),
        out_shape=[jax.ShapeDtypeStruct((m, d), dt) for dt in out_dtypes],
        grid=(m // tm,),
        in_specs=[
            row_spec,
            pl.BlockSpec((SUBLANES, d), lambda i: (jnp.maximum(i * blk - 1, 0), 0)),
            pl.BlockSpec((1, d), lambda i: (0, 0)),
            full(mu), full(first),
        ] + [full(s) for s in seconds] + [full(b) for b in biases],
        out_specs=[row_spec] * len(out_dtypes),
        compiler_params=_params(("parallel",)),
        name="rwkv_mix",
    )(x, x, g.reshape(1, d), mu, first, *seconds, *biases)
    return outs


def _scan_kernel(*refs, has_vres, n_chunks, n_pairs):
    if has_vres:
        (r_ref, k_ref, v_ref, w_ref, a_ref, g_ref, vf_ref, vg_ref,
         kk_ref, ka_ref, rk_ref, lw_ref, lb_ref, o_ref, ht_ref) = refs
    else:
        (r_ref, k_ref, v_ref, w_ref, a_ref, g_ref,
         kk_ref, ka_ref, rk_ref, lw_ref, lb_ref, o_ref, ht_ref) = refs
    c2 = 2 * CHUNK

    @pl.when(pl.program_id(2) == 0)
    def _():
        ht_ref[...] = jnp.zeros_like(ht_ref)

    lane = lax.broadcasted_iota(jnp.int32, (CHUNK, LANES), 1)
    head0 = lane < HEAD

    def both(x, y):
        return jnp.concatenate([x, y], axis=0)

    def split(x0, x1):
        return jnp.where(head0, x0, x1)

    ri = lax.broadcasted_iota(jnp.int32, (LANES, LANES), 0)
    ci = lax.broadcasted_iota(jnp.int32, (LANES, LANES), 1)

    def head_sum(x):
        s0 = jnp.sum(jnp.where(head0, x, 0.0), axis=-1, keepdims=True)
        s1 = jnp.sum(jnp.where(head0, 0.0, x), axis=-1, keepdims=True)
        return jnp.where(head0, s0, s1)

    eye = (ri == ci).astype(F32)
    same_head = (ri // HEAD) == (ci // HEAD)
    ti = lax.broadcasted_iota(jnp.int32, (CHUNK, CHUNK), 0)
    si = lax.broadcasted_iota(jnp.int32, (CHUNK, CHUNK), 1)
    tri = (ti >= si).astype(BF16)
    ts = lax.broadcasted_iota(jnp.int32, (CHUNK, LANES), 0)
    strict = ts > lane % HEAD
    incl = ts >= lane % HEAD

    ops, ys = {}, {}

    def head(p, c):
        rows = slice(c * CHUNK, (c + 1) * CHUNK)
        cols = slice(p * LANES, (p + 1) * LANES)
        r, k, v, a = r_ref[rows, cols], k_ref[rows, cols], v_ref[rows, cols], a_ref[rows, cols]
        k_k, k_a, r_k = kk_ref[:, cols], ka_ref[:, cols], rk_ref[:, cols]
        if has_vres:
            v = v + (vf_ref[rows, cols] - v) * vg_ref[rows, cols]
        lw = -DECAY_SCALE * jax.nn.sigmoid(w_ref[rows, cols])
        lp = _dot(tri, _split_hl(lw))
        kk = k * k_k
        norm = jnp.sqrt(head_sum(kk * kk))
        yield
        kmod = k * (1.0 + (a - 1.0) * k_a)
        bonus = head_sum(r * kmod * r_k)
        yield
        lp = lp[:, :LANES] + lp[:, LANES:]
        lp_end = lp[CHUNK - 1:CHUNK, :]
        en = jnp.exp(-lp)
        to_end = jnp.exp(lp_end - lp)
        r_t = r * jnp.exp(lp)
        yield
        kk = kk / jnp.maximum(norm, 1e-12)
        bb = kk * a
        a_t = -kk * jnp.exp(lp - lw)
        b_t = (bb * en).astype(BF16)
        k_t = (kmod * en).astype(BF16)
        v_b = v.astype(BF16)
        yield
        ops[p, c] = dict(
            lhs0=both(jnp.where(head0, a_t, 0.0), jnp.where(head0, r_t, 0.0)).astype(BF16),
            lhs1=both(jnp.where(head0, 0.0, a_t), jnp.where(head0, 0.0, r_t)).astype(BF16),
            rhs0=both(b_t, k_t), rhs1=both(k_t, b_t),
            ar=both(a_t, r_t).astype(BF16), v_b=v_b,
            bk_end=both((bb * to_end).astype(BF16), (kmod * to_end).astype(BF16)),
            decay_end=jnp.exp(lp_end), bonus_v=bonus * v)
        yield

    def mid(p, c):
        o = ops.pop((p, c))
        v_b = o["v_b"]
        g0 = lax.dot_general(o["lhs0"], o["rhs0"], _NT, preferred_element_type=F32)
        g1 = lax.dot_general(o["lhs1"], o["rhs1"], _NT, preferred_element_type=F32)
        yield
        top0 = jnp.where(strict, g0[:CHUNK, :], 0.0)
        top1 = jnp.where(strict, g1[:CHUNK, :], 0.0)
        bot0 = jnp.where(incl, g0[CHUNK:, :], 0.0).astype(BF16)
        bot1 = jnp.where(incl, g1[CHUNK:, :], 0.0).astype(BF16)
        n_ab = both(jnp.where(head0, top0, 0.0), jnp.where(head0, 0.0, top1))
        n_ak = both(jnp.where(head0, 0.0, top0), jnp.where(head0, top1, 0.0)).astype(BF16)

        inv = eye + n_ab
        pw = n_ab.astype(BF16)
        pw = _dot(pw, pw).astype(BF16)
        yield
        for _ in range(CHUNK.bit_length() - 3):
            prod = _dot(pw, jnp.concatenate([pw, inv.astype(BF16)], axis=1))
            yield
            pw = prod[:, :c2].astype(BF16)
            inv = inv + prod[:, c2:]
        inv = inv + _dot(pw, inv.astype(BF16))

        ht = ht_ref[p]
        state_proj = lax.dot_general(o["ar"], ht.astype(BF16), _NT, preferred_element_type=F32)
        xv = _dot(n_ak, both(v_b, v_b))
        yield
        sp_a = state_proj[:CHUNK, :]
        xs = both(jnp.where(head0, xv[:CHUNK, :] + sp_a, 0.0), jnp.where(head0, 0.0, xv[CHUNK:, :] + sp_a))
        us = _dot(inv.astype(BF16), xs.astype(BF16))
        yield
        u_b = (us[:CHUNK, :] + us[CHUNK:, :]).astype(BF16)
        y = state_proj[CHUNK:, :] + split(_dot(bot0, both(u_b, v_b)), _dot(bot1, both(v_b, u_b)))
        new = lax.dot_general(both(u_b, v_b), o["bk_end"], _TN, preferred_element_type=F32)
        ht_ref[p] = ht * o["decay_end"] + jnp.where(same_head, new, 0.0)
        yield
        ys[p, c] = (y, o["bonus_v"])
        yield

    def tail(p, c):
        rows = slice(c * CHUNK, (c + 1) * CHUNK)
        cols = slice(p * LANES, (p + 1) * LANES)
        y, bonus_v = ys.pop((p, c))
        mean = head_sum(y) * (1.0 / HEAD)
        yield
        yc = y - mean
        var = head_sum(yc * yc) * (1.0 / HEAD)
        yield
        out = yc * lax.rsqrt(var + GN_EPS) * lw_ref[:, cols] + lb_ref[:, cols] + bonus_v
        o_ref[rows, cols] = (out * g_ref[rows, cols]).astype(o_ref.dtype)
        yield

    done = object()

    def pipeline(p):
        yield from head(p, 0)
        for c in range(n_chunks):
            main = mid(p, c)
            sides = ([head(p, c + 1)] if c + 1 < n_chunks else []) + ([tail(p, c - 1)] if c > 0 else [])
            step, busy = 0, True
            while busy:
                busy = next(main, done) is not done
                if sides:
                    side = sides[step % len(sides)]
                    if next(side, done) is done:
                        sides.remove(side)
                    else:
                        busy = True
                step += 1
                yield
        yield from tail(p, n_chunks - 1)

    programs = [pipeline(p) for p in range(n_pairs)]
    lag = [0 if p < n_pairs // 2 else STAGGER for p in range(n_pairs)]
    step, live = 0, n_pairs
    while live:
        live = 0
        for p, prog in enumerate(programs):
            if step >= lag[p] and next(prog, done) is not done:
                live += 1
            elif step < lag[p]:
                live += 1
        step += 1


def _rwkv_scan(r, k, v, w, a, g, k_k, k_a, r_k, ln_w, ln_b, seq_len, vres=None, *, tc=256, n_pairs=16):
    m, d = r.shape
    tc = _tile(seq_len, tc)
    nt = seq_len // tc
    n_pairs = _tile(d // LANES, n_pairs)
    width = n_pairs * LANES
    row_spec = pl.BlockSpec((tc, width), lambda b, h, c: (b * nt + c, h))
    par_spec = pl.BlockSpec((1, width), lambda b, h, c: (0, h))
    seqs = [r, k, v, w, a, g] + (list(vres) if vres is not None else [])
    pars = [p.reshape(1, d) for p in (k_k, k_a, r_k, ln_w, ln_b)]
    return pl.pallas_call(
        functools.partial(_scan_kernel, has_vres=vres is not None, n_chunks=tc // CHUNK, n_pairs=n_pairs),
        out_shape=jax.ShapeDtypeStruct((m, d), BF16),
        grid=(m // seq_len, d // width, nt),
        in_specs=[row_spec] * len(seqs) + [par_spec] * len(pars),
        out_specs=row_spec,
        scratch_shapes=[pltpu.VMEM((n_pairs, LANES, LANES), F32)],
        compiler_params=_params(("parallel", "parallel", "arbitrary")),
        name="rwkv_scan",
    )(*seqs, *pars)


def _swa_kernel(sink_ref, q_ref, kp_ref, kc_ref, vp_ref, vc_ref, o_ref, *, n_kv, q_per_kv, blocks_per_seq):
    first = pl.program_id(0) % blocks_per_seq == 0
    lane = lax.broadcasted_iota(jnp.int32, (ATT_BLOCK, LANES), 1)
    even = lane < HEAD
    qi = lax.broadcasted_iota(jnp.int32, (ATT_BLOCK, 2 * ATT_BLOCK), 0)
    kc = lax.broadcasted_iota(jnp.int32, (ATT_BLOCK, 2 * ATT_BLOCK), 1)
    valid = (kc > qi) & (kc <= qi + ATT_BLOCK) & ((kc >= ATT_BLOCK) | jnp.logical_not(first))
    bias = jnp.where(valid, 0.0, -jnp.inf)
    for gi in range(n_kv):
        cols = slice(gi * LANES, (gi + 1) * LANES)
        keys = jnp.concatenate([kp_ref[:, cols], kc_ref[:, cols]], axis=0)
        vals = jnp.concatenate([vp_ref[:, cols], vc_ref[:, cols]], axis=0)

        def head_pair(tile):
            q2 = q_ref[:, tile * LANES:(tile + 1) * LANES] * (HEAD ** -0.5)
            sinks = [sink_ref[2 * tile + half] for half in range(2)]
            logits = [lax.dot_general(jnp.where(even if half == 0 else jnp.logical_not(even), q2, jnp.zeros_like(q2)),
                                      keys, _NT, preferred_element_type=F32) for half in range(2)]
            yield
            logits = [lg + bias for lg in logits]
            mx = [jnp.maximum(jnp.max(lg, axis=-1, keepdims=True), s) for lg, s in zip(logits, sinks)]
            yield
            p = [jnp.exp(lg - m) for lg, m in zip(logits, mx)]
            denom = [jnp.sum(pp, axis=-1, keepdims=True) + jnp.exp(s - m) for pp, s, m in zip(p, sinks, mx)]
            yield
            outs = [_dot((pp * (1.0 / dn)).astype(BF16), vals) for pp, dn in zip(p, denom)]
            yield
            o_ref[:, tile * LANES:(tile + 1) * LANES] = jnp.where(even, outs[0], outs[1]).astype(o_ref.dtype)

        tiles = [head_pair(gi * (q_per_kv // 2) + pi) for pi in range(q_per_kv // 2)]
        done = object()
        while any([next(t, done) is not done for t in tiles]):
            pass


def _swa(qkv, sinks, seq_len, d, n_kv):
    m = qkv.shape[0]
    n_q = d // HEAD
    q_tiles = d // LANES
    blocks_per_seq = seq_len // ATT_BLOCK
    kw = n_kv * LANES
    k_blk = d // kw
    prev = lambda i: jnp.maximum(i - 1, 0)
    return pl.pallas_call(
        functools.partial(_swa_kernel, n_kv=n_kv, q_per_kv=n_q // n_kv, blocks_per_seq=blocks_per_seq),
        out_shape=jax.ShapeDtypeStruct((m, d), BF16),
        grid=(m // ATT_BLOCK,),
        in_specs=[
            pl.BlockSpec(memory_space=pltpu.SMEM),
            pl.BlockSpec((ATT_BLOCK, d), lambda i: (i, 0)),
            pl.BlockSpec((ATT_BLOCK, kw), lambda i: (prev(i), k_blk)),
            pl.BlockSpec((ATT_BLOCK, kw), lambda i: (i, k_blk)),
            pl.BlockSpec((ATT_BLOCK, kw), lambda i: (prev(i), k_blk + 1)),
            pl.BlockSpec((ATT_BLOCK, kw), lambda i: (i, k_blk + 1)),
        ],
        out_specs=pl.BlockSpec((ATT_BLOCK, d), lambda i: (i, 0)),
        compiler_params=_params(("parallel",)),
        name="swa",
    )(sinks, qkv, qkv, qkv, qkv, qkv)


def _conv_kernel(x_ref, p_ref, g_ref, wb_ref, wc_ref, wh_ref, cw_ref, o_ref, xn_ref, *, tiles_per_seq):
    @pl.when(pl.program_id(1) == 0)
    def _():
        g = g_ref[...]
        xn_ref[:SUBLANES, :] = _rms(p_ref[...], g).astype(BF16)
        xn_ref[SUBLANES:, :] = _rms(x_ref[...], g).astype(BF16)

    xn = xn_ref[...]
    u = _dot(xn, wc_ref[...]) * _dot(xn, wh_ref[...])
    row = lax.broadcasted_iota(jnp.int32, u.shape, 0)
    first = pl.program_id(0) % tiles_per_seq == 0
    u = jnp.where((row < SUBLANES) & first, 0.0, u)
    cw = cw_ref[...]
    uc = (cw[2:3, :] * u + cw[1:2, :] * pltpu.roll(u, 1, 0) + cw[0:1, :] * pltpu.roll(u, 2, 0))[SUBLANES:, :]
    bg = _dot(xn[SUBLANES:, :], wb_ref[...])
    o_ref[...] = (bg * uc).astype(o_ref.dtype)


def _conv_front(x, g, w_in, conv_w, layer, seq_len, *, tm=512, tn=2048):
    m, d = x.shape
    tm, tn = _tile(seq_len, tm), _tile(d, tn)
    nj = d // tn
    blk = tm // SUBLANES
    resident = dict(pipeline_mode=pl.Buffered(1)) if nj == 1 else {}
    return pl.pallas_call(
        functools.partial(_conv_kernel, tiles_per_seq=seq_len // tm),
        out_shape=jax.ShapeDtypeStruct((m, d), BF16),
        grid=(m // tm, nj),
        in_specs=[
            pl.BlockSpec((tm, d), lambda i, j: (i, 0)),
            pl.BlockSpec((SUBLANES, d), lambda i, j: (jnp.maximum(i * blk - 1, 0), 0)),
            pl.BlockSpec((1, d), lambda i, j: (0, 0)),
            pl.BlockSpec((None, d, tn), lambda i, j: (layer, 0, j), **resident),
            pl.BlockSpec((None, d, tn), lambda i, j: (layer, 0, nj + j), **resident),
            pl.BlockSpec((None, d, tn), lambda i, j: (layer, 0, 2 * nj + j), **resident),
            pl.BlockSpec((None, conv_w.shape[1], tn), lambda i, j: (layer, 0, j)),
        ],
        out_specs=pl.BlockSpec((tm, tn), lambda i, j: (i, j)),
        scratch_shapes=[pltpu.VMEM((SUBLANES + tm, d), BF16)],
        compiler_params=_params(("parallel", "arbitrary")),
        name="conv_front",
    )(x, x, g.reshape(1, d), w_in, w_in, w_in, conv_w)


def _rwkv_layer(x, seq_len, g_norm, mu, w_rkv, w_o, layer, w0, w1, w2, a0, a1, a2, g1, g2,
                k_k, k_a, r_k, ln_w, ln_b, v_first, vres):
    outs = _rwkv_mix(x, g_norm, mu, seq_len, w0, w1, w2, a0, a1, a2, g1, g2, vres)
    xr, xk, xv, w, a, g = outs[:6]
    r = _mm(xr, w_rkv, w_lead=(layer, 0), name="rwkv_r")
    k = _mm(xk, w_rkv, w_lead=(layer, 1), name="rwkv_k")
    v = _mm(xv, w_rkv, w_lead=(layer, 2), name="rwkv_v")
    if vres is None:
        scan_vres = None
        v_first = v
    else:
        scan_vres = (v_first, outs[6])
    yg = _rwkv_scan(r, k, v, w, a, g, k_k, k_a, r_k.reshape(-1), ln_w, ln_b, seq_len, scan_vres)
    return _mm(yg, w_o, w_lead=(layer,), res=x, name="rwkv_o"), v_first


def _swa_layer(x, seq_len, g_norm, w_qkv, b_qkv, w_o, layer, b_o, sinks):
    d = x.shape[1]
    n_kv = (w_qkv.shape[1] - d) // (2 * HEAD)
    kvd = n_kv * HEAD

    def dup(t):
        lead = t.shape[:-1]
        t = t.reshape(lead + (n_kv, 1, HEAD))
        return jnp.broadcast_to(t, lead + (n_kv, 2, HEAD)).reshape(lead + (2 * kvd,))

    w_cat = jnp.concatenate([w_qkv[:, :d], dup(w_qkv[:, d:d + kvd]), dup(w_qkv[:, d + kvd:])], axis=1)
    b_cat = jnp.concatenate([b_qkv[:d], dup(b_qkv[d:d + kvd]), dup(b_qkv[d + kvd:])])
    qkv = _mm(x, w_cat.astype(BF16), norm_g=g_norm, bias=b_cat, out_dtype=BF16, name="swa_qkv")
    o = _swa(qkv, sinks, seq_len, d, n_kv)
    return _mm(o, w_o, w_lead=(layer,), bias=b_o, res=x, name="swa_o")


def _conv_layer(x, seq_len, g_norm, w_in, conv_w, w_out, layer):
    gated = _conv_front(x, g_norm, w_in, conv_w, layer, seq_len)
    return _mm(gated, w_out, w_lead=(layer,), res=x, name="conv_o")


def kernel(x, norm_mix, norm_ffn, norm_final, rwkv_mu, rwkv_w_rkv, rwkv_w_o, rwkv_w0, rwkv_w1, rwkv_w2, rwkv_a0, rwkv_a1, rwkv_a2, rwkv_v0, rwkv_v1, rwkv_v2, rwkv_g1, rwkv_g2, rwkv_k_k, rwkv_k_a, rwkv_r_k, rwkv_lnx_w, rwkv_lnx_b, swa_w_qkv, swa_b_qkv, swa_w_o, swa_b_o, swa_sinks, conv_w_in, conv_w, conv_w_out, mlp_w_up, mlp_w_down):
    bsz, seq_len, d = x.shape
    depth = norm_mix.shape[0]
    xf = x.reshape(bsz * seq_len, d)
    rwkv_w_rkv, rwkv_w_o, swa_w_o, conv_w_in, conv_w_out, mlp_w_up, mlp_w_down = (
        t.astype(BF16) for t in (rwkv_w_rkv, rwkv_w_o, swa_w_o, conv_w_in, conv_w_out, mlp_w_up, mlp_w_down))
    v_first = None
    ia = ib = ic = 0
    for i in range(depth):
        kind = i % 3
        if kind == 0:
            vres = None if ia == 0 else (rwkv_v0[ia - 1], rwkv_v1[ia - 1], rwkv_v2[ia - 1])
            xf, v_first = _rwkv_layer(
                xf, seq_len, norm_mix[i], rwkv_mu[ia], rwkv_w_rkv, rwkv_w_o, ia,
                rwkv_w0[ia], rwkv_w1[ia], rwkv_w2[ia], rwkv_a0[ia], rwkv_a1[ia], rwkv_a2[ia],
                rwkv_g1[ia], rwkv_g2[ia], rwkv_k_k[ia], rwkv_k_a[ia], rwkv_r_k[ia],
                rwkv_lnx_w[ia], rwkv_lnx_b[ia], v_first, vres)
            ia += 1
        elif kind == 1:
            xf = _swa_layer(xf, seq_len, norm_mix[i], swa_w_qkv[ib], swa_b_qkv[ib],
                            swa_w_o, ib, swa_b_o[ib], swa_sinks[ib])
            ib += 1
        else:
            xf = _conv_layer(xf, seq_len, norm_mix[i], conv_w_in, conv_w, conv_w_out, ic)
            ic += 1
        xf = _ffn(xf, norm_ffn[i], mlp_w_up, mlp_w_down, i, norm_final if i == depth - 1 else None)
    return xf.reshape(bsz, seq_len, d)
```

```python
import functools

import jax
import jax.numpy as jnp
from jax import lax
from jax.experimental import pallas as pl
from jax.experimental.pallas import tpu as pltpu

F32 = jnp.float32
BF16 = jnp.bfloat16

HEAD = 64
LANES = 128
SUBLANES = 8
CHUNK = 64
ATT_BLOCK = 128
STAGGER = 6
RMS_EPS = 1e-6
GN_EPS = 64e-5
DECAY_SCALE = 0.6065306597126334
VMEM_LIMIT = 56 * 1024 * 1024

_NT = (((1,), (1,)), ((), ()))
_TN = (((0,), (0,)), ((), ()))


def _tile(n, pref):
    t = min(n, pref)
    while n % t:
        t //= 2
    return t


def _params(sem):
    return pltpu.CompilerParams(dimension_semantics=sem, vmem_limit_bytes=VMEM_LIMIT)


def _rms(x, g):
    ms = jnp.mean(x * x, axis=-1, keepdims=True)
    return x * lax.rsqrt(ms + RMS_EPS) * g


def _dot(a, b):
    return jnp.dot(a, b, preferred_element_type=F32)


def _split_hl(x):
    hi = x.astype(BF16)
    lo = (x - hi.astype(F32)).astype(BF16)
    return jnp.concatenate([hi, lo], axis=1)


def _mm_kernel(*refs, has_norm, has_bias, has_res):
    it = iter(refs)
    x_ref = next(it)
    g_ref = next(it) if has_norm else None
    w_ref = next(it)
    b_ref = next(it) if has_bias else None
    r_ref = next(it) if has_res else None
    o_ref = next(it)
    xn_ref = next(it) if has_norm else None

    if has_norm:
        @pl.when(pl.program_id(1) == 0)
        def _():
            xn_ref[...] = _rms(x_ref[...], g_ref[...]).astype(BF16)
        xb = xn_ref[...]
    else:
        xb = x_ref[...]
    acc = _dot(xb, w_ref[...])
    if has_bias:
        acc = acc + b_ref[...]
    if has_res:
        acc = acc + r_ref[...]
    o_ref[...] = acc.astype(o_ref.dtype)


def _mm(x, w, *, w_lead=(), norm_g=None, bias=None, res=None, out_dtype=F32, tm=1024, tn=4096, name="mm"):
    m, k = x.shape
    n = w.shape[-1]
    squeezed = (None,) * len(w_lead)
    tm, tn = _tile(m, tm), _tile(n, tn)
    has_norm, has_bias, has_res = norm_g is not None, bias is not None, res is not None
    resident = dict(pipeline_mode=pl.Buffered(1)) if tn == n else {}
    args, specs = [x], [pl.BlockSpec((tm, k), lambda i, j: (i, 0))]
    if has_norm:
        args.append(norm_g.reshape(1, k))
        specs.append(pl.BlockSpec((1, k), lambda i, j: (0, 0)))
    args.append(w)
    specs.append(pl.BlockSpec(squeezed + (k, tn), lambda i, j: w_lead + (0, j), **resident))
    if has_bias:
        args.append(bias.reshape(1, n))
        specs.append(pl.BlockSpec((1, tn), lambda i, j: (0, j)))
    if has_res:
        args.append(res)
        specs.append(pl.BlockSpec((tm, tn), lambda i, j: (i, j)))
    return pl.pallas_call(
        functools.partial(_mm_kernel, has_norm=has_norm, has_bias=has_bias, has_res=has_res),
        out_shape=jax.ShapeDtypeStruct((m, n), out_dtype),
        grid=(m // tm, n // tn),
        in_specs=specs,
        out_specs=pl.BlockSpec((tm, tn), lambda i, j: (i, j)),
        scratch_shapes=[pltpu.VMEM((tm, k), BF16)] if has_norm else [],
        compiler_params=_params(("parallel", "arbitrary")),
        name=name,
    )(*args)


def _ffn_kernel(*refs, has_final):
    if has_final:
        x_ref, g_ref, wu_ref, wd_ref, gf_ref, o_ref, xn_ref = refs
    else:
        x_ref, g_ref, wu_ref, wd_ref, o_ref, xn_ref = refs
    f = pl.program_id(1)

    @pl.when(f == 0)
    def _():
        x = x_ref[...]
        xn_ref[...] = _rms(x, g_ref[...]).astype(BF16)
        o_ref[...] = x

    h = jnp.maximum(_dot(xn_ref[...], wu_ref[...]), 0.0)
    o_ref[...] += _dot((h * h).astype(BF16), wd_ref[...])

    if has_final:
        @pl.when(f == pl.num_programs(1) - 1)
        def _():
            o_ref[...] = _rms(o_ref[...], gf_ref[...])


def _ffn(x, g, w_up, w_down, layer, final_g=None, *, tm=512, tf=1024):
    m, d = x.shape
    ff = w_up.shape[-1]
    tm, tf = _tile(m, tm), _tile(ff, tf)
    has_final = final_g is not None
    args = [x, g.reshape(1, d), w_up, w_down]
    specs = [
        pl.BlockSpec((tm, d), lambda i, f: (i, 0)),
        pl.BlockSpec((1, d), lambda i, f: (0, 0)),
        pl.BlockSpec((None, d, tf), lambda i, f: (layer, 0, f)),
        pl.BlockSpec((None, tf, d), lambda i, f: (layer, f, 0)),
    ]
    if has_final:
        args.append(final_g.reshape(1, d))
        specs.append(pl.BlockSpec((1, d), lambda i, f: (0, 0)))
    return pl.pallas_call(
        functools.partial(_ffn_kernel, has_final=has_final),
        out_shape=jax.ShapeDtypeStruct((m, d), F32),
        grid=(m // tm, ff // tf),
        in_specs=specs,
        out_specs=pl.BlockSpec((tm, d), lambda i, f: (i, 0)),
        scratch_shapes=[pltpu.VMEM((tm, d), BF16)],
        compiler_params=_params(("parallel", "arbitrary")),
        name="ffn",
    )(*args)


def _mix_kernel(*refs, tiles_per_seq, has_vres, ranks):
    it = iter(refs)
    x_ref, p_ref, g_ref, mu_ref, a_ref = (next(it) for _ in range(5))
    w2_ref, a2_ref, g2_ref = next(it), next(it), next(it)
    v2_ref = next(it) if has_vres else None
    w0_ref, a0_ref = next(it), next(it)
    v0_ref = next(it) if has_vres else None
    xr_ref, xk_ref, xv_ref, w_ref, ao_ref, go_ref = (next(it) for _ in range(6))
    vg_ref = next(it) if has_vres else None

    g = g_ref[...]
    h = _rms(x_ref[...], g)
    hp = _rms(p_ref[...], g)[SUBLANES - 1:SUBLANES, :]
    hp = jnp.where(pl.program_id(0) % tiles_per_seq == 0, 0.0, hp)
    row = lax.broadcasted_iota(jnp.int32, h.shape, 0)
    shifted = jnp.where(row == 0, hp, pltpu.roll(h, 1, 0))
    xx = shifted - h

    def mix(c):
        return (h + xx * mu_ref[c:c + 1, :]).astype(BF16)

    rw, ra, rg, rv = ranks

    def branch(c, lo, hi, mid_act, second_ref, bias_ref, out_act, out_ref, x_out_ref=None):
        xm = mix(c)
        if x_out_ref is not None:
            x_out_ref[...] = xm
        mid = _dot(xm, a_ref[:, lo:hi])
        yield
        out = _dot(mid_act(mid).astype(BF16), second_ref[...])
        yield
        if bias_ref is not None:
            out = out + bias_ref[...]
        out_ref[...] = out_act(out).astype(out_ref.dtype)

    def plain(c, x_out_ref):
        x_out_ref[...] = mix(c)
        yield

    ident = lambda t: t
    branches = [
        branch(1, 0, rw, jnp.tanh, w2_ref, w0_ref, ident, w_ref),
        plain(0, xr_ref),
        branch(4, rw, rw + ra, ident, a2_ref, a0_ref, jax.nn.sigmoid, ao_ref),
        plain(2, xk_ref),
        branch(5, rw + ra, rw + ra + rg, jax.nn.sigmoid, g2_ref, None, ident, go_ref),
    ]
    if has_vres:
        branches.append(branch(3, rw + ra + rg, rw + ra + rg + rv, ident, v2_ref, v0_ref, jax.nn.sigmoid, vg_ref, xv_ref))
    else:
        branches.append(plain(3, xv_ref))
    done = object()
    while any([next(b, done) is not done for b in branches]):
        pass


def _pad_rank(a, b):
    pad = -a.shape[1] % LANES
    return jnp.pad(a, ((0, 0), (0, pad))).astype(BF16), jnp.pad(b, ((0, pad), (0, 0))).astype(BF16)


def _rwkv_mix(x, g, mu, seq_len, w0, w1, w2, a0, a1, a2, g1, g2, vres, *, tm=256):
    m, d = x.shape
    tm = _tile(seq_len, tm)
    blk = tm // SUBLANES
    has_vres = vres is not None
    pairs = [_pad_rank(w1, w2), _pad_rank(a1, a2), _pad_rank(g1, g2)]
    biases = [w0, a0]
    if has_vres:
        pairs.append(_pad_rank(vres[1], vres[2]))
        biases.append(vres[0])
    first = jnp.concatenate([p[0] for p in pairs], axis=1)
    ranks = tuple(p[0].shape[1] for p in pairs) + ((0,) if not has_vres else ())
    row_spec = pl.BlockSpec((tm, d), lambda i: (i, 0))
    full = lambda arr: pl.BlockSpec(arr.shape, lambda i: (0, 0))
    seconds = [p[1] for p in pairs]
    biases = [b.reshape(1, d) for b in biases]
    out_dtypes = [BF16, BF16, BF16, F32, F32, BF16] + ([F32] if has_vres else [])
    outs = pl.pallas_call(
        functools.partial(_mix_kernel, tiles_per_seq=seq_len // tm, has_vres=has_vres, ranks=ranks),
        out_shape=[jax.ShapeDtypeStruct((m, d), dt) for dt in out_dtypes],
        grid=(m // tm,),
        in_specs=[
            row_spec,
            pl.BlockSpec((SUBLANES, d), lambda i: (jnp.maximum(i * blk - 1, 0), 0)),
            pl.BlockSpec((1, d), lambda i: (0, 0)),
            full(mu), full(first),
        ] + [full(s) for s in seconds] + [full(b) for b in biases],
        out_specs=[row_spec] * len(out_dtypes),
        compiler_params=_params(("parallel",)),
        name="rwkv_mix",
    )(x, x, g.reshape(1, d), mu, first, *seconds, *biases)
    return outs


def _scan_kernel(*refs, has_vres, n_chunks, n_pairs):
    if has_vres:
        (r_ref, k_ref, v_ref, w_ref, a_ref, g_ref, vf_ref, vg_ref,
         kk_ref, ka_ref, rk_ref, lw_ref, lb_ref, o_ref, ht_ref) = refs
    else:
        (r_ref, k_ref, v_ref, w_ref, a_ref, g_ref,
         kk_ref, ka_ref, rk_ref, lw_ref, lb_ref, o_ref, ht_ref) = refs

    @pl.when(pl.program_id(2) == 0)
    def _():
        ht_ref[...] = jnp.zeros_like(ht_ref)

    lane = lax.broadcasted_iota(jnp.int32, (CHUNK, LANES), 1)
    head0 = lane < HEAD

    def both(x, y):
        return jnp.concatenate([x, y], axis=0)

    def split(x0, x1):
        return jnp.where(head0, x0, x1)

    ri = lax.broadcasted_iota(jnp.int32, (LANES, LANES), 0)
    ci = lax.broadcasted_iota(jnp.int32, (LANES, LANES), 1)

    def head_sum(x):
        s0 = jnp.sum(jnp.where(head0, x, 0.0), axis=-1, keepdims=True)
        s1 = jnp.sum(jnp.where(head0, 0.0, x), axis=-1, keepdims=True)
        return jnp.where(head0, s0, s1)

    same_head = (ri // HEAD) == (ci // HEAD)
    ti = lax.broadcasted_iota(jnp.int32, (CHUNK, CHUNK), 0)
    si = lax.broadcasted_iota(jnp.int32, (CHUNK, CHUNK), 1)
    tri = (ti >= si).astype(BF16)
    ts = lax.broadcasted_iota(jnp.int32, (CHUNK, LANES), 0)
    strict = ts > lane % HEAD
    incl = ts >= lane % HEAD
    eye_right = (lane == ts + HEAD).astype(F32)

    ops, ys = {}, {}

    def head(p, c):
        rows = slice(c * CHUNK, (c + 1) * CHUNK)
        cols = slice(p * LANES, (p + 1) * LANES)
        r, k, v, a = r_ref[rows, cols], k_ref[rows, cols], v_ref[rows, cols], a_ref[rows, cols]
        k_k, k_a, r_k = kk_ref[:, cols], ka_ref[:, cols], rk_ref[:, cols]
        if has_vres:
            v = v + (vf_ref[rows, cols] - v) * vg_ref[rows, cols]
        lw = -DECAY_SCALE * jax.nn.sigmoid(w_ref[rows, cols])
        lp = _dot(tri, _split_hl(lw))
        kk = k * k_k
        norm = jnp.sqrt(head_sum(kk * kk))
        yield
        kmod = k * (1.0 + (a - 1.0) * k_a)
        bonus = head_sum(r * kmod * r_k)
        yield
        lp = lp[:, :LANES] + lp[:, LANES:]
        lp_end = lp[CHUNK - 1:CHUNK, :]
        en = jnp.exp(-lp)
        to_end = jnp.exp(lp_end - lp)
        r_t = r * jnp.exp(lp)
        yield
        kk = kk / jnp.maximum(norm, 1e-12)
        bb = kk * a
        a_t = -kk * jnp.exp(lp - lw)
        b_t = (bb * en).astype(BF16)
        k_t = (kmod * en).astype(BF16)
        v_b = v.astype(BF16)
        yield
        ops[p, c] = dict(
            lhs0=both(jnp.where(head0, a_t, 0.0), jnp.where(head0, r_t, 0.0)).astype(BF16),
            lhs1=both(jnp.where(head0, 0.0, a_t), jnp.where(head0, 0.0, r_t)).astype(BF16),
            rhs=both(b_t, k_t),
            ar=both(a_t, r_t).astype(BF16), v_b=v_b,
            bk_end=both((bb * to_end).astype(BF16), (kmod * to_end).astype(BF16)),
            decay_end=jnp.exp(lp_end), bonus_v=bonus * v)
        yield

    def mid(p, c):
        o = ops.pop((p, c))
        v_b = o["v_b"]
        vv = both(v_b, v_b)
        grams = [lax.dot_general(o[lhs], o["rhs"], _NT, preferred_element_type=F32) for lhs in ("lhs0", "lhs1")]
        yield
        tops = [jnp.where(strict, gm[:CHUNK, :], 0.0) for gm in grams]
        bots = both(*[jnp.where(incl, gm[CHUNK:, :], 0.0).astype(BF16) for gm in grams])

        zs = [jnp.where(head0, tp, eye_right) for tp in tops]
        for _ in range(CHUNK.bit_length() - 1):
            zb = [z.astype(BF16) for z in zs]
            prods = [_dot(z, both(z, jnp.zeros_like(z))) for z in zb]
            yield
            zs = [pr + jnp.where(head0, 0.0, z) for pr, z in zip(prods, zs)]

        ht = ht_ref[p]
        state_proj = lax.dot_general(o["ar"], ht.astype(BF16), _NT, preferred_element_type=F32)
        xv = _dot(both(*[jnp.where(head0, 0.0, tp) for tp in tops]).astype(BF16), vv)
        yield
        sp_a = state_proj[:CHUNK, :]
        xs = [(xv[:CHUNK, :] + sp_a).astype(BF16), (xv[CHUNK:, :] + sp_a).astype(BF16)]
        us = [_dot(z.astype(BF16), both(jnp.zeros_like(x), x)) for z, x in zip(zs, xs)]
        yield
        uv = both(split(us[0], us[1]).astype(BF16), v_b)
        res = _dot(bots, uv)
        y = state_proj[CHUNK:, :] + split(res[:CHUNK, :], res[CHUNK:, :])
        new = lax.dot_general(uv, o["bk_end"], _TN, preferred_element_type=F32)
        ht_ref[p] = ht * o["decay_end"] + jnp.where(same_head, new, 0.0)
        yield
        ys[p, c] = (y, o["bonus_v"])
        yield

    def tail(p, c):
        rows = slice(c * CHUNK, (c + 1) * CHUNK)
        cols = slice(p * LANES, (p + 1) * LANES)
        y, bonus_v = ys.pop((p, c))
        mean = head_sum(y) * (1.0 / HEAD)
        yield
        yc = y - mean
        var = head_sum(yc * yc) * (1.0 / HEAD)
        yield
        out = yc * lax.rsqrt(var + GN_EPS) * lw_ref[:, cols] + lb_ref[:, cols] + bonus_v
        o_ref[rows, cols] = (out * g_ref[rows, cols]).astype(o_ref.dtype)
        yield

    done = object()

    def pipeline(p):
        yield from head(p, 0)
        for c in range(n_chunks):
            main = mid(p, c)
            sides = ([head(p, c + 1)] if c + 1 < n_chunks else []) + ([tail(p, c - 1)] if c > 0 else [])
            step, busy = 0, True
            while busy:
                busy = next(main, done) is not done
                if sides:
                    side = sides[step % len(sides)]
                    if next(side, done) is done:
                        sides.remove(side)
                    else:
                        busy = True
                step += 1
                yield
        yield from tail(p, n_chunks - 1)

    programs = [pipeline(p) for p in range(n_pairs)]
    lag = [0 if p < n_pairs // 2 else STAGGER for p in range(n_pairs)]
    step, live = 0, n_pairs
    while live:
        live = 0
        for p, prog in enumerate(programs):
            if step >= lag[p] and next(prog, done) is not done:
                live += 1
            elif step < lag[p]:
                live += 1
        step += 1


def _rwkv_scan(r, k, v, w, a, g, k_k, k_a, r_k, ln_w, ln_b, seq_len, vres=None, *, tc=256, n_pairs=16):
    m, d = r.shape
    tc = _tile(seq_len, tc)
    nt = seq_len // tc
    n_pairs = _tile(d // LANES, n_pairs)
    width = n_pairs * LANES
    row_spec = pl.BlockSpec((tc, width), lambda b, h, c: (b * nt + c, h))
    par_spec = pl.BlockSpec((1, width), lambda b, h, c: (0, h))
    seqs = [r, k, v, w, a, g] + (list(vres) if vres is not None else [])
    pars = [p.reshape(1, d) for p in (k_k, k_a, r_k, ln_w, ln_b)]
    return pl.pallas_call(
        functools.partial(_scan_kernel, has_vres=vres is not None, n_chunks=tc // CHUNK, n_pairs=n_pairs),
        out_shape=jax.ShapeDtypeStruct((m, d), BF16),
        grid=(m // seq_len, d // width, nt),
        in_specs=[row_spec] * len(seqs) + [par_spec] * len(pars),
        out_specs=row_spec,
        scratch_shapes=[pltpu.VMEM((n_pairs, LANES, LANES), F32)],
        compiler_params=_params(("parallel", "parallel", "arbitrary")),
        name="rwkv_scan",
    )(*seqs, *pars)


def _swa_kernel(sink_ref, q_ref, kp_ref, kc_ref, vp_ref, vc_ref, o_ref, *, n_kv, q_per_kv, blocks_per_seq):
    first = pl.program_id(0) % blocks_per_seq == 0
    lane = lax.broadcasted_iota(jnp.int32, (ATT_BLOCK, LANES), 1)
    even = lane < HEAD
    qi = lax.broadcasted_iota(jnp.int32, (ATT_BLOCK, 2 * ATT_BLOCK), 0)
    kc = lax.broadcasted_iota(jnp.int32, (ATT_BLOCK, 2 * ATT_BLOCK), 1)
    valid = (kc > qi) & (kc <= qi + ATT_BLOCK) & ((kc >= ATT_BLOCK) | jnp.logical_not(first))
    bias = jnp.where(valid, 0.0, -jnp.inf)
    for gi in range(n_kv):
        cols = slice(gi * LANES, (gi + 1) * LANES)
        keys = jnp.concatenate([kp_ref[:, cols], kc_ref[:, cols]], axis=0)
        vals = jnp.concatenate([vp_ref[:, cols], vc_ref[:, cols]], axis=0)

        def head_pair(tile):
            q2 = q_ref[:, tile * LANES:(tile + 1) * LANES] * (HEAD ** -0.5)
            sinks = [sink_ref[2 * tile + half] for half in range(2)]
            logits = [lax.dot_general(jnp.where(even if half == 0 else jnp.logical_not(even), q2, jnp.zeros_like(q2)),
                                      keys, _NT, preferred_element_type=F32) for half in range(2)]
            yield
            logits = [lg + bias for lg in logits]
            mx = [jnp.maximum(jnp.max(lg, axis=-1, keepdims=True), s) for lg, s in zip(logits, sinks)]
            yield
            p = [jnp.exp(lg - m) for lg, m in zip(logits, mx)]
            denom = [jnp.sum(pp, axis=-1, keepdims=True) + jnp.exp(s - m) for pp, s, m in zip(p, sinks, mx)]
            yield
            outs = [_dot((pp * (1.0 / dn)).astype(BF16), vals) for pp, dn in zip(p, denom)]
            yield
            o_ref[:, tile * LANES:(tile + 1) * LANES] = jnp.where(even, outs[0], outs[1]).astype(o_ref.dtype)

        tiles = [head_pair(gi * (q_per_kv // 2) + pi) for pi in range(q_per_kv // 2)]
        done = object()
        while any([next(t, done) is not done for t in tiles]):
            pass


def _swa(qkv, sinks, seq_len, d, n_kv):
    m = qkv.shape[0]
    n_q = d // HEAD
    q_tiles = d // LANES
    blocks_per_seq = seq_len // ATT_BLOCK
    kw = n_kv * LANES
    k_blk = d // kw
    prev = lambda i: jnp.maximum(i - 1, 0)
    return pl.pallas_call(
        functools.partial(_swa_kernel, n_kv=n_kv, q_per_kv=n_q // n_kv, blocks_per_seq=blocks_per_seq),
        out_shape=jax.ShapeDtypeStruct((m, d), BF16),
        grid=(m // ATT_BLOCK,),
        in_specs=[
            pl.BlockSpec(memory_space=pltpu.SMEM),
            pl.BlockSpec((ATT_BLOCK, d), lambda i: (i, 0)),
            pl.BlockSpec((ATT_BLOCK, kw), lambda i: (prev(i), k_blk)),
            pl.BlockSpec((ATT_BLOCK, kw), lambda i: (i, k_blk)),
            pl.BlockSpec((ATT_BLOCK, kw), lambda i: (prev(i), k_blk + 1)),
            pl.BlockSpec((ATT_BLOCK, kw), lambda i: (i, k_blk + 1)),
        ],
        out_specs=pl.BlockSpec((ATT_BLOCK, d), lambda i: (i, 0)),
        compiler_params=_params(("parallel",)),
        name="swa",
    )(sinks, qkv, qkv, qkv, qkv, qkv)


def _conv_kernel(x_ref, p_ref, g_ref, wb_ref, wc_ref, wh_ref, cw_ref, o_ref, xn_ref, *, tiles_per_seq):
    @pl.when(pl.program_id(1) == 0)
    def _():
        g = g_ref[...]
        xn_ref[:SUBLANES, :] = _rms(p_ref[...], g).astype(BF16)
        xn_ref[SUBLANES:, :] = _rms(x_ref[...], g).astype(BF16)

    xn = xn_ref[...]
    u = _dot(xn, wc_ref[...]) * _dot(xn, wh_ref[...])
    row = lax.broadcasted_iota(jnp.int32, u.shape, 0)
    first = pl.program_id(0) % tiles_per_seq == 0
    u = jnp.where((row < SUBLANES) & first, 0.0, u)
    cw = cw_ref[...]
    uc = (cw[2:3, :] * u + cw[1:2, :] * pltpu.roll(u, 1, 0) + cw[0:1, :] * pltpu.roll(u, 2, 0))[SUBLANES:, :]
    bg = _dot(xn[SUBLANES:, :], wb_ref[...])
    o_ref[...] = (bg * uc).astype(o_ref.dtype)


def _conv_front(x, g, w_in, conv_w, layer, seq_len, *, tm=512, tn=2048):
    m, d = x.shape
    tm, tn = _tile(seq_len, tm), _tile(d, tn)
    nj = d // tn
    blk = tm // SUBLANES
    resident = dict(pipeline_mode=pl.Buffered(1)) if nj == 1 else {}
    return pl.pallas_call(
        functools.partial(_conv_kernel, tiles_per_seq=seq_len // tm),
        out_shape=jax.ShapeDtypeStruct((m, d), BF16),
        grid=(m // tm, nj),
        in_specs=[
            pl.BlockSpec((tm, d), lambda i, j: (i, 0)),
            pl.BlockSpec((SUBLANES, d), lambda i, j: (jnp.maximum(i * blk - 1, 0), 0)),
            pl.BlockSpec((1, d), lambda i, j: (0, 0)),
            pl.BlockSpec((None, d, tn), lambda i, j: (layer, 0, j), **resident),
            pl.BlockSpec((None, d, tn), lambda i, j: (layer, 0, nj + j), **resident),
            pl.BlockSpec((None, d, tn), lambda i, j: (layer, 0, 2 * nj + j), **resident),
            pl.BlockSpec((None, conv_w.shape[1], tn), lambda i, j: (layer, 0, j)),
        ],
        out_specs=pl.BlockSpec((tm, tn), lambda i, j: (i, j)),
        scratch_shapes=[pltpu.VMEM((SUBLANES + tm, d), BF16)],
        compiler_params=_params(("parallel", "arbitrary")),
        name="conv_front",
    )(x, x, g.reshape(1, d), w_in, w_in, w_in, conv_w)


def _rwkv_layer(x, seq_len, g_norm, mu, w_rkv, w_o, layer, w0, w1, w2, a0, a1, a2, g1, g2,
                k_k, k_a, r_k, ln_w, ln_b, v_first, vres):
    outs = _rwkv_mix(x, g_norm, mu, seq_len, w0, w1, w2, a0, a1, a2, g1, g2, vres)
    xr, xk, xv, w, a, g = outs[:6]
    r = _mm(xr, w_rkv, w_lead=(layer, 0), name="rwkv_r")
    k = _mm(xk, w_rkv, w_lead=(layer, 1), name="rwkv_k")
    v = _mm(xv, w_rkv, w_lead=(layer, 2), name="rwkv_v")
    if vres is None:
        scan_vres = None
        v_first = v
    else:
        scan_vres = (v_first, outs[6])
    yg = _rwkv_scan(r, k, v, w, a, g, k_k, k_a, r_k.reshape(-1), ln_w, ln_b, seq_len, scan_vres)
    return _mm(yg, w_o, w_lead=(layer,), res=x, name="rwkv_o"), v_first


def _swa_layer(x, seq_len, g_norm, w_qkv, b_qkv, w_o, layer, b_o, sinks):
    d = x.shape[1]
    n_kv = (w_qkv.shape[1] - d) // (2 * HEAD)
    kvd = n_kv * HEAD

    def dup(t):
        lead = t.shape[:-1]
        t = t.reshape(lead + (n_kv, 1, HEAD))
        return jnp.broadcast_to(t, lead + (n_kv, 2, HEAD)).reshape(lead + (2 * kvd,))

    w_cat = jnp.concatenate([w_qkv[:, :d], dup(w_qkv[:, d:d + kvd]), dup(w_qkv[:, d + kvd:])], axis=1)
    b_cat = jnp.concatenate([b_qkv[:d], dup(b_qkv[d:d + kvd]), dup(b_qkv[d + kvd:])])
    qkv = _mm(x, w_cat.astype(BF16), norm_g=g_norm, bias=b_cat, out_dtype=BF16, name="swa_qkv")
    o = _swa(qkv, sinks, seq_len, d, n_kv)
    return _mm(o, w_o, w_lead=(layer,), bias=b_o, res=x, name="swa_o")


def _conv_layer(x, seq_len, g_norm, w_in, conv_w, w_out, layer):
    gated = _conv_front(x, g_norm, w_in, conv_w, layer, seq_len)
    return _mm(gated, w_out, w_lead=(layer,), res=x, name="conv_o")


def kernel(x, norm_mix, norm_ffn, norm_final, rwkv_mu, rwkv_w_rkv, rwkv_w_o, rwkv_w0, rwkv_w1, rwkv_w2, rwkv_a0, rwkv_a1, rwkv_a2, rwkv_v0, rwkv_v1, rwkv_v2, rwkv_g1, rwkv_g2, rwkv_k_k, rwkv_k_a, rwkv_r_k, rwkv_lnx_w, rwkv_lnx_b, swa_w_qkv, swa_b_qkv, swa_w_o, swa_b_o, swa_sinks, conv_w_in, conv_w, conv_w_out, mlp_w_up, mlp_w_down):
    bsz, seq_len, d = x.shape
    depth = norm_mix.shape[0]
    xf = x.reshape(bsz * seq_len, d)
    rwkv_w_rkv, rwkv_w_o, swa_w_o, conv_w_in, conv_w_out, mlp_w_up, mlp_w_down = (
        t.astype(BF16) for t in (rwkv_w_rkv, rwkv_w_o, swa_w_o, conv_w_in, conv_w_out, mlp_w_up, mlp_w_down))
    v_first = None
    ia = ib = ic = 0
    for i in range(depth):
        kind = i % 3
        if kind == 0:
            vres = None if ia == 0 else (rwkv_v0[ia - 1], rwkv_v1[ia - 1], rwkv_v2[ia - 1])
            xf, v_first = _rwkv_layer(
                xf, seq_len, norm_mix[i], rwkv_mu[ia], rwkv_w_rkv, rwkv_w_o, ia,
                rwkv_w0[ia], rwkv_w1[ia], rwkv_w2[ia], rwkv_a0[ia], rwkv_a1[ia], rwkv_a2[ia],
                rwkv_g1[ia], rwkv_g2[ia], rwkv_k_k[ia], rwkv_k_a[ia], rwkv_r_k[ia],
                rwkv_lnx_w[ia], rwkv_lnx_b[ia], v_first, vres)
            ia += 1
        elif kind == 1:
            xf = _swa_layer(xf, seq_len, norm_mix[i], swa_w_qkv[ib], swa_b_qkv[ib],
                            swa_w_o, ib, swa_b_o[ib], swa_sinks[ib])
            ib += 1
        else:
            xf = _conv_layer(xf, seq_len, norm_mix[i], conv_w_in, conv_w, conv_w_out, ic)
            ic += 1
        xf = _ffn(xf, norm_ffn[i], mlp_w_up, mlp_w_down, i, norm_final if i == depth - 1 else None)
    return xf.reshape(bsz, seq_len, d)
```
